```python
import math
import jax, jax.numpy as jnp
from jax import lax
import numpy as np

D_MODEL = 2048
BATCH = 4
SEQ = 8192
DEPTH = 2

GRID_W = 64
CTX_LEN = 256

GDN_HEADS = 8
GDN_DK = 128
GDN_DV = 128
GDN_CONV = 5
GDN_CHUNK_LOG2 = 6
GDN_CHUNK = 2 ** GDN_CHUNK_LOG2
GDN_QK_W = GDN_HEADS * GDN_DK
GDN_V_W = GDN_HEADS * GDN_DV
GDN_QKV_W = 2 * GDN_QK_W + GDN_V_W

SWA_Q_HEADS = 16
SWA_KV_HEADS = 4
SWA_HEAD_DIM = 64
SWA_WINDOW = 128
SWA_BLOCK = 128
ROPE_THETA = 10000.0
SWA_Q_W = SWA_Q_HEADS * SWA_HEAD_DIM
SWA_KV_W = SWA_KV_HEADS * SWA_HEAD_DIM

SGU_GROUPS = 8
SGU_CHUNK = 128
SGU_W = D_MODEL // 2

D_FF = 4 * D_MODEL
N_MOD = 6
DEEPNORM_ALPHA = (2 * DEPTH) ** 0.25
DEEPNORM_BETA = (8 * DEPTH) ** -0.25
LN_EPS = 1e-5
RMS_EPS = 1e-6

IN_SPLITS = (GDN_QKV_W, GDN_V_W, GDN_HEADS, GDN_HEADS, GDN_HEADS, GDN_HEADS,
             SWA_Q_W, SWA_KV_W, SWA_KV_W, SGU_W, SGU_W, D_MODEL, D_MODEL, D_MODEL)
IN_SPLIT_IDX = tuple(int(i) for i in np.cumsum(IN_SPLITS)[:-1])
D_IN = sum(IN_SPLITS)

kernel_name = 'hybrid_gdn_swa_sgu_diffusion_block'


def layer_norm(x, g, b):
    xf = x.astype(jnp.float32)
    mu = jnp.mean(xf, axis=-1, keepdims=True)
    var = jnp.mean(jnp.square(xf - mu), axis=-1, keepdims=True)
    return ((xf - mu) * lax.rsqrt(var + LN_EPS)).astype(x.dtype) * g + b


def l2norm(x):
    return x * lax.rsqrt(jnp.sum(jnp.square(x), axis=-1, keepdims=True) + RMS_EPS)


def modulate(x, shift, scale):
    return x * (1.0 + scale) + shift


def depthwise_conv(x, w):
    k_width, n_ch = w.shape
    return lax.conv_general_dilated(x, w[:, None, :], window_strides=(1,),
                                    padding=((k_width // 2, k_width // 2),),
                                    dimension_numbers=('NWC', 'WIO', 'NWC'),
                                    feature_group_count=n_ch)


def axial_rope(t_len):
    rows = t_len // GRID_W
    row = jnp.broadcast_to(jnp.arange(rows)[:, None], (rows, GRID_W)).reshape(t_len).astype(jnp.float32)
    col = jnp.broadcast_to(jnp.arange(GRID_W)[None, :], (rows, GRID_W)).reshape(t_len).astype(jnp.float32)
    n_freq = SWA_HEAD_DIM // 4
    freq = jnp.power(ROPE_THETA, -jnp.arange(n_freq, dtype=jnp.float32) / n_freq)
    ang = jnp.concatenate([row[:, None] * freq, col[:, None] * freq], axis=-1)[:, None, :]
    return jnp.cos(ang), jnp.sin(ang)


def apply_rope(x, cos, sin):
    x1, x2 = jnp.split(x.astype(jnp.float32), 2, axis=-1)
    return jnp.concatenate([x1 * cos - x2 * sin, x2 * cos + x1 * sin], axis=-1).astype(x.dtype)


def gdn_features(qkv, beta_f, beta_b, a_f, a_b, conv_w, a_log, dt_bias):
    bsz, t_len, _ = qkv.shape
    qkv = jax.nn.silu(depthwise_conv(qkv, conv_w)).astype(jnp.float32)
    q, k, v = jnp.split(qkv, (GDN_QK_W, 2 * GDN_QK_W), axis=-1)
    q = l2norm(q.reshape(bsz, t_len, GDN_HEADS, GDN_DK)) * (GDN_DK ** -0.5)
    k = l2norm(k.reshape(bsz, t_len, GDN_HEADS, GDN_DK))
    v = v.reshape(bsz, t_len, GDN_HEADS, GDN_DV)
    dirs = []
    for d, (b_in, a_in) in enumerate(((beta_f, a_f), (beta_b, a_b))):
        g = -jnp.exp(a_log[d].astype(jnp.float32)) * jax.nn.softplus((a_in + dt_bias[d]).astype(jnp.float32))
        dirs.append((g, jax.nn.sigmoid(b_in.astype(jnp.float32))))
    return q, k, v, dirs


def gdn_chunked(q, k, v, g, beta, state):
    bsz, t_len, n_h, _ = q.shape
    c_len = GDN_CHUNK
    n_chunk = t_len // c_len

    def chunks(t):
        return jnp.moveaxis(t.reshape((bsz, n_chunk, c_len, n_h) + t.shape[3:]), 3, 1)

    q, k, v, g, beta = chunks(q), chunks(k), chunks(v), chunks(g), chunks(beta)
    cum_g = jnp.cumsum(g, axis=-1)
    incl = jnp.tril(jnp.ones((c_len, c_len), dtype=bool))
    strict = jnp.tril(jnp.ones((c_len, c_len), dtype=bool), -1)
    diff = cum_g[..., :, None] - cum_g[..., None, :]
    decay_mat = jnp.where(incl, jnp.exp(jnp.where(incl, diff, 0.0)), 0.0)
    k_beta = k * beta[..., None]
    neg_m = -jnp.where(strict, jnp.einsum('bhnid,bhnjd->bhnij', k_beta, k) * decay_mat, 0.0)
    inv = jnp.eye(c_len, dtype=q.dtype) + neg_m
    power = neg_m
    for _ in range(GDN_CHUNK_LOG2 - 1):
        power = power @ power
        inv = inv + inv @ power
    u = inv @ (v * beta[..., None])
    w = inv @ (k_beta * jnp.exp(cum_g)[..., None])
    a_qk = jnp.where(incl, jnp.einsum('bhnid,bhnjd->bhnij', q, k) * decay_mat, 0.0)
    q_dec = q * jnp.exp(cum_g)[..., None]
    g_last = cum_g[..., -1:]
    k_dec = k * jnp.exp(g_last - cum_g)[..., None]
    state_dec = jnp.exp(g_last[..., 0])

    def step(s, xs):
        u_n, w_n, a_n, q_n, k_n, d_n = xs
        v_new = u_n - jnp.einsum('bhck,bhkv->bhcv', w_n, s)
        o_n = jnp.einsum('bhck,bhkv->bhcv', q_n, s) + jnp.einsum('bhcs,bhsv->bhcv', a_n, v_new)
        s = s * d_n[..., None, None] + jnp.einsum('bhck,bhcv->bhkv', k_n, v_new)
        return s, o_n

    xs = tuple(jnp.moveaxis(t, 2, 0) for t in (u, w, a_qk, q_dec, k_dec, state_dec))
    state, o = lax.scan(step, state, xs)
    return jnp.transpose(o, (1, 0, 3, 2, 4)).reshape(bsz, t_len, n_h, GDN_DV), state


def gdn_bidirectional(ctx_feats, lat_feats):
    q_c, k_c, v_c, dirs_c = ctx_feats
    q_l, k_l, v_l, dirs_l = lat_feats
    bsz = q_c.shape[0]
    o_c = 0.0
    o_l = 0.0
    for d in range(2):
        rev = (lambda t: jnp.flip(t, axis=1)) if d == 1 else (lambda t: t)
        g_c, b_c = dirs_c[d]
        g_l, b_l = dirs_l[d]
        s0 = jnp.zeros((bsz, GDN_HEADS, GDN_DK, GDN_DV), jnp.float32)
        oc, s_ctx = gdn_chunked(rev(q_c), rev(k_c), rev(v_c), rev(g_c), rev(b_c), s0)
        ol, _ = gdn_chunked(rev(q_l), rev(k_l), rev(v_l), rev(g_l), rev(b_l), s_ctx)
        o_c = o_c + rev(oc)
        o_l = o_l + rev(ol)
    return o_l, o_c


def gated_rmsnorm(o, z, w):
    bsz, t_len = z.shape[:2]
    on = o * lax.rsqrt(jnp.mean(jnp.square(o), axis=-1, keepdims=True) + RMS_EPS) * w.astype(jnp.float32)
    zf = jax.nn.silu(z.reshape(bsz, t_len, GDN_HEADS, GDN_DV).astype(jnp.float32))
    return (on * zf).reshape(bsz, t_len, GDN_V_W).astype(z.dtype)


def swa_latent(q, k, v, k_ctx, v_ctx, sinks):
    bsz, t_len, n_q, hd = q.shape
    n_blk = t_len // SWA_BLOCK
    n_g = SWA_KV_HEADS
    n_r = n_q // n_g
    scale = hd ** -0.5
    qb = q.reshape(bsz, n_blk, SWA_BLOCK, n_g, n_r, hd)

    def band(t):
        tp = jnp.pad(t, ((0, 0), (SWA_BLOCK, SWA_BLOCK), (0, 0), (0, 0))).reshape(bsz, n_blk + 2, SWA_BLOCK, n_g, hd)
        return jnp.concatenate([tp[:, :-2], tp[:, 1:-1], tp[:, 2:]], axis=2)

    kb, vb = band(k), band(v)
    s_win = jnp.einsum('bnqgrd,bnkgd->bgrnqk', qb, kb, preferred_element_type=jnp.float32) * scale
    key_off = jnp.arange(3 * SWA_BLOCK) - SWA_BLOCK
    rel = key_off[None, :] - jnp.arange(SWA_BLOCK)[:, None]
    key_pos = jnp.arange(n_blk)[:, None, None] * SWA_BLOCK + key_off[None, None, :]
    mask = (jnp.abs(rel) <= SWA_WINDOW)[None] & (key_pos >= 0) & (key_pos < t_len)
    s_win = jnp.where(mask, s_win, -jnp.inf)
    s_ctx = jnp.einsum('bnqgrd,bcgd->bgrnqc', qb, k_ctx, preferred_element_type=jnp.float32) * scale
    sink = sinks.astype(jnp.float32).reshape(n_g, n_r)[None, :, :, None, None, None]
    m = jnp.maximum(jnp.maximum(jnp.max(s_win, axis=-1, keepdims=True), jnp.max(s_ctx, axis=-1, keepdims=True)), sink)
    p_win = jnp.exp(s_win - m)
    p_ctx = jnp.exp(s_ctx - m)
    inv = 1.0 / (jnp.sum(p_win, axis=-1, keepdims=True) + jnp.sum(p_ctx, axis=-1, keepdims=True) + jnp.exp(sink - m))
    o = (jnp.einsum('bgrnqk,bnkgd->bnqgrd', p_win.astype(v.dtype), vb)
         + jnp.einsum('bgrnqc,bcgd->bnqgrd', p_ctx.astype(v.dtype), v_ctx))
    o = o * jnp.transpose(inv, (0, 3, 4, 1, 2, 5))
    return o.reshape(bsz, t_len, n_q * hd).astype(v.dtype)


def swa_context(q, k, v, sinks):
    bsz, c_len, n_q, hd = q.shape
    n_g = SWA_KV_HEADS
    n_r = n_q // n_g
    qg = q.reshape(bsz, c_len, n_g, n_r, hd)
    s = jnp.einsum('bqgrd,bkgd->bgrqk', qg, k, preferred_element_type=jnp.float32) * (hd ** -0.5)
    sink = jnp.broadcast_to(sinks.astype(jnp.float32).reshape(n_g, n_r)[None, :, :, None, None], (bsz, n_g, n_r, c_len, 1))
    p = jax.nn.softmax(jnp.concatenate([s, sink], axis=-1), axis=-1)[..., :-1]
    o = jnp.einsum('bgrqk,bkgd->bqgrd', p.astype(v.dtype), v)
    return o.reshape(bsz, c_len, n_q * hd)


def spatial_gating(u, v, ln_g, ln_b, w_s, b_s):
    bsz, t_len, width = u.shape
    n_chunk = t_len // SGU_CHUNK
    u = jax.nn.gelu(u)
    v = layer_norm(jax.nn.gelu(v), ln_g, ln_b)
    vb = v.reshape(bsz, n_chunk, SGU_CHUNK, SGU_GROUPS, width // SGU_GROUPS)
    s = jnp.einsum('gpq,bnqgc->bnpgc', w_s, vb) + b_s.T[:, :, None]
    return u * s.reshape(bsz, t_len, width)


def merge_branches(gate_a, gate_b, gate_c, y_a, y_b, y_c, w_a, w_b, w_c, w_out):
    y = (jax.nn.sigmoid(gate_a) * (y_a @ w_a) + jax.nn.sigmoid(gate_b) * (y_b @ w_b)
         + jax.nn.sigmoid(gate_c) * (y_c @ w_c))
    return y @ w_out


def token_mixer(h, hc, cos, sin, w_in, b_in, conv_w, a_log, dt_bias, gdn_norm_w, sinks,
                sgu_ln_g, sgu_ln_b, sgu_w, sgu_b, w_a, w_b, w_c, w_out, need_ctx):
    p_l = jnp.split(h @ w_in + b_in, IN_SPLIT_IDX, axis=-1)
    p_c = jnp.split(hc @ w_in + b_in, IN_SPLIT_IDX, axis=-1)

    feats_l = gdn_features(p_l[0], p_l[2], p_l[3], p_l[4], p_l[5], conv_w, a_log, dt_bias)
    feats_c = gdn_features(p_c[0], p_c[2], p_c[3], p_c[4], p_c[5], conv_w, a_log, dt_bias)
    o_a_l, o_a_c = gdn_bidirectional(feats_c, feats_l)
    y_a_l = gated_rmsnorm(o_a_l, p_l[1], gdn_norm_w)

    def heads(t, n):
        return t.reshape(t.shape[0], t.shape[1], n, SWA_HEAD_DIM)

    q_b_l = apply_rope(heads(p_l[6], SWA_Q_HEADS), cos, sin)
    k_b_l = apply_rope(heads(p_l[7], SWA_KV_HEADS), cos, sin)
    v_b_l = heads(p_l[8], SWA_KV_HEADS)
    k_b_c = heads(p_c[7], SWA_KV_HEADS)
    v_b_c = heads(p_c[8], SWA_KV_HEADS)
    y_b_l = swa_latent(q_b_l, k_b_l, v_b_l, k_b_c, v_b_c, sinks)

    y_c_l = spatial_gating(p_l[9], p_l[10], sgu_ln_g, sgu_ln_b, sgu_w, sgu_b)

    y_lat = merge_branches(p_l[11], p_l[12], p_l[13], y_a_l, y_b_l, y_c_l, w_a, w_b, w_c, w_out)
    if not need_ctx:
        return y_lat, None
    y_a_c = gated_rmsnorm(o_a_c, p_c[1], gdn_norm_w)
    y_b_c = swa_context(heads(p_c[6], SWA_Q_HEADS), k_b_c, v_b_c, sinks)
    y_c_c = spatial_gating(p_c[9], p_c[10], sgu_ln_g, sgu_ln_b, sgu_w, sgu_b)
    y_ctx = merge_branches(p_c[11], p_c[12], p_c[13], y_a_c, y_b_c, y_c_c, w_a, w_b, w_c, w_out)
    return y_lat, y_ctx


def squared_relu_mlp(h, w1, w2):
    return jnp.square(jax.nn.relu(h @ w1)) @ w2


def setup_inputs(seed: int = 0) -> dict:
    key = jax.random.key(seed)
    keys = iter(jax.random.split(key, 40))

    def normal(shape, scale):
        return jax.random.normal(next(keys), shape, jnp.float32) * scale

    def gain(shape):
        return 1.0 + normal(shape, 0.02)

    nl = DEPTH
    dt = jnp.exp(jax.random.uniform(next(keys), (nl, 2, GDN_HEADS), jnp.float32, math.log(1e-3), math.log(1e-1)))
    a_log = jnp.log(jax.random.uniform(next(keys), (nl, 2, GDN_HEADS), jnp.float32, 1.0, 16.0))
    return {
        'x': normal((BATCH, SEQ, D_MODEL), 1.0),
        'c': normal((BATCH, D_MODEL), 1.0),
        'ctx': normal((BATCH, CTX_LEN, D_MODEL), 1.0),
        'c_ctx': normal((D_MODEL,), 1.0),
        'w_ada': normal((nl, D_MODEL, N_MOD * D_MODEL), 0.5 * D_MODEL ** -0.5),
        'b_ada': normal((nl, N_MOD * D_MODEL), 0.02),
        'w_in': normal((nl, D_MODEL, D_IN), D_MODEL ** -0.5),
        'b_in': normal((nl, D_IN), 0.02),
        'gdn_conv': normal((nl, GDN_CONV, GDN_QKV_W), GDN_CONV ** -0.5),
        'gdn_a_log': a_log,
        'gdn_dt_bias': dt + jnp.log(-jnp.expm1(-dt)),
        'gdn_norm': gain((nl, GDN_DV)),
        'swa_sinks': normal((nl, SWA_Q_HEADS), 0.5),
        'sgu_ln_g': gain((nl, SGU_W)),
        'sgu_ln_b': normal((nl, SGU_W), 0.02),
        'sgu_w': normal((nl, SGU_GROUPS, SGU_CHUNK, SGU_CHUNK), SGU_CHUNK ** -0.5),
        'sgu_b': gain((nl, SGU_GROUPS, SGU_CHUNK)),
        'w_branch_a': normal((nl, GDN_V_W, D_MODEL), DEEPNORM_BETA * GDN_V_W ** -0.5),
        'w_branch_b': normal((nl, SWA_Q_W, D_MODEL), DEEPNORM_BETA * SWA_Q_W ** -0.5),
        'w_branch_c': normal((nl, SGU_W, D_MODEL), DEEPNORM_BETA * SGU_W ** -0.5),
        'w_out': normal((nl, D_MODEL, D_MODEL), DEEPNORM_BETA * D_MODEL ** -0.5),
        'ln_mix_g': gain((nl, D_MODEL)),
        'ln_mix_b': normal((nl, D_MODEL), 0.02),
        'w_ff1': normal((nl, D_MODEL, D_FF), D_MODEL ** -0.5),
        'w_ff2': normal((nl, D_FF, D_MODEL), DEEPNORM_BETA * D_FF ** -0.5),
        'ln_ff_g': gain((nl, D_MODEL)),
        'ln_ff_b': normal((nl, D_MODEL), 0.02),
    }


def reference(x, c, ctx, c_ctx, w_ada, b_ada, w_in, b_in, gdn_conv, gdn_a_log, gdn_dt_bias, gdn_norm,
              swa_sinks, sgu_ln_g, sgu_ln_b, sgu_w, sgu_b, w_branch_a, w_branch_b, w_branch_c, w_out,
              ln_mix_g, ln_mix_b, w_ff1, w_ff2, ln_ff_g, ln_ff_b):
    t_len = x.shape[1]
    cos, sin = axial_rope(t_len)
    c_act = jax.nn.silu(c)
    cc_act = jax.nn.silu(c_ctx)
    xc = ctx
    for l in range(DEPTH):
        need_ctx = l < DEPTH - 1
        mod = jnp.split((c_act @ w_ada[l] + b_ada[l])[:, None, :], N_MOD, axis=-1)
        mod_c = jnp.split(cc_act @ w_ada[l] + b_ada[l], N_MOD, axis=-1)
        y, y_c = token_mixer(modulate(x, mod[0], mod[1]), modulate(xc, mod_c[0], mod_c[1]), cos, sin,
                             w_in[l], b_in[l], gdn_conv[l], gdn_a_log[l], gdn_dt_bias[l], gdn_norm[l],
                             swa_sinks[l], sgu_ln_g[l], sgu_ln_b[l], sgu_w[l], sgu_b[l],
                             w_branch_a[l], w_branch_b[l], w_branch_c[l], w_out[l], need_ctx)
        x = layer_norm(DEEPNORM_ALPHA * x + mod[2] * y, ln_mix_g[l], ln_mix_b[l])
        f = squared_relu_mlp(modulate(x, mod[3], mod[4]), w_ff1[l], w_ff2[l])
        x = layer_norm(DEEPNORM_ALPHA * x + mod[5] * f, ln_ff_g[l], ln_ff_b[l])
        if need_ctx:
            xc = layer_norm(DEEPNORM_ALPHA * xc + mod_c[2] * y_c, ln_mix_g[l], ln_mix_b[l])
            f_c = squared_relu_mlp(modulate(xc, mod_c[3], mod_c[4]), w_ff1[l], w_ff2[l])
            xc = layer_norm(DEEPNORM_ALPHA * xc + mod_c[5] * f_c, ln_ff_g[l], ln_ff_b[l])
    return x
```

```python
import functools
import math

import jax
import jax.numpy as jnp
from jax import lax
from jax.experimental import pallas as pl
from jax.experimental.pallas import tpu as pltpu

F32 = jnp.float32
BF16 = jnp.bfloat16

GRID_W = 64
GDN_HEADS = 8
GDN_DK = 128
GDN_DV = 128
GDN_CONV = 5
SWA_Q_HEADS = 16
SWA_KV_HEADS = 4
SWA_HEAD_DIM = 64
SWA_BLOCK = 128
ROPE_THETA = 10000.0
SGU_GROUPS = 8
SGU_CHUNK = 128
N_MOD = 6
LN_EPS = 1e-5
RMS_EPS = 1e-6

GDN_QK_W = GDN_HEADS * GDN_DK
GDN_V_W = GDN_HEADS * GDN_DV
GDN_QKV_W = 2 * GDN_QK_W + GDN_V_W
SWA_Q_W = SWA_Q_HEADS * SWA_HEAD_DIM
SWA_KV_W = SWA_KV_HEADS * SWA_HEAD_DIM

LANES = 128
SUBLANES = 8
VMEM_LIMIT_BYTES = 56 * 1024 * 1024

GDN_CHUNK = 128
GDN_SERIES_BLOCK = 16
NEG_BIG = -1e30


def _cparams(*sem):
    return pltpu.CompilerParams(dimension_semantics=sem, vmem_limit_bytes=VMEM_LIMIT_BYTES)


def _dot(a, b):
    return jnp.dot(a, b, preferred_element_type=F32)


def _dot_nt(a, b):
    return lax.dot_general(a, b, (((1,), (1,)), ((), ())), preferred_element_type=F32)


def _layer_norm_rows(r, g, b):
    mu = jnp.mean(r, axis=-1, keepdims=True)
    d = r - mu
    var = jnp.mean(d * d, axis=-1, keepdims=True)
    return d * lax.rsqrt(var + LN_EPS) * g + b


def _gelu_tanh(x):
    return x * (0.5 * (1.0 + jnp.tanh(math.sqrt(2.0 / math.pi) * (x + 0.044715 * (x * x * x)))))


def _pick_tile(n, pref):
    t = min(n, pref)
    while n % t:
        t //= 2
    return t


def _ada_kernel(c_ref, w_ref, b_ref, o_ref):
    c = c_ref[...]
    a = (c * jax.nn.sigmoid(c)).astype(BF16)
    o_ref[0] = _dot(a, w_ref[0].astype(BF16)) + b_ref[0]


def _ada_mod(cc, w_ada, b_ada):
    nl, d, n = w_ada.shape
    tn = _pick_tile(n, 1024)
    return pl.pallas_call(
        _ada_kernel,
        out_shape=jax.ShapeDtypeStruct((nl, cc.shape[0], n), F32),
        grid=(nl, n // tn),
        in_specs=[pl.BlockSpec(cc.shape, lambda l, j: (0, 0)),
                  pl.BlockSpec((1, d, tn), lambda l, j: (l, 0, j)),
                  pl.BlockSpec((1, 1, tn), lambda l, j: (l, 0, j))],
        out_specs=pl.BlockSpec((1, cc.shape[0], tn), lambda l, j: (l, 0, j)),
        compiler_params=_cparams("arbitrary", "arbitrary"),
        name="ada_mod",
    )(cc, w_ada, b_ada.reshape(nl, 1, n))


def _inproj_kernel(x_ref, mod_ref, w_ref, b_ref, o_ref, h_scr):
    @pl.when(pl.program_id(2) == 0)
    def _():
        shift = mod_ref[0, 0:1, :]
        scale = mod_ref[0, 1:2, :]
        h_scr[...] = (x_ref[0] * (1.0 + scale) + shift).astype(BF16)

    o_ref[0] = _dot(h_scr[...], w_ref[...]) + b_ref[...]


def _inproj(x, mod, w, b):
    bsz, t_len, d = x.shape
    n = w.shape[1]
    tm = _pick_tile(t_len, 512)
    tn = _pick_tile(n, 1536)
    per_batch = mod.shape[0] > 1
    return pl.pallas_call(
        _inproj_kernel,
        out_shape=jax.ShapeDtypeStruct((bsz, t_len, n), F32),
        grid=(bsz, t_len // tm, n // tn),
        in_specs=[pl.BlockSpec((1, tm, d), lambda bi, i, j: (bi, i, 0)),
                  pl.BlockSpec((1, N_MOD, d), (lambda bi, i, j: (bi, 0, 0)) if per_batch else (lambda bi, i, j: (0, 0, 0))),
                  pl.BlockSpec((d, tn), lambda bi, i, j: (0, j)),
                  pl.BlockSpec((1, tn), lambda bi, i, j: (0, j))],
        out_specs=pl.BlockSpec((1, tm, tn), lambda bi, i, j: (bi, i, j)),
        scratch_shapes=[pltpu.VMEM((tm, d), BF16)],
        compiler_params=_cparams("arbitrary", "arbitrary", "arbitrary"),
        name="in_proj",
    )(x, mod, w, b)


def _gdn_gate_kernel(x_ref, mod_ref, wt_ref, bt_ref, alog_ref, dtb_ref, o_ref):
    shift = mod_ref[0, 0:1, :]
    scale = mod_ref[0, 1:2, :]
    h = (x_ref[0] * (1.0 + scale) + shift).astype(BF16)
    st = _dot_nt(wt_ref[...], h) + bt_ref[...]
    beta = jax.nn.sigmoid(st)
    xs = st + dtb_ref[...]
    softplus = jnp.maximum(xs, 0.0) + jnp.log1p(jnp.exp(-jnp.abs(xs)))
    g = -jnp.exp(alog_ref[...]) * softplus
    row = lax.broadcasted_iota(jnp.int32, st.shape, 0)
    o_ref[0] = jnp.where(row < 2 * GDN_HEADS, beta, g)


def _gdn_gates(x, mod, wt, bt, alog, dtb):
    bsz, t_len, d = x.shape
    r = wt.shape[0]
    tm = _pick_tile(t_len, 512)
    per_batch = mod.shape[0] > 1
    small = lambda bi, i: (0, 0)
    return pl.pallas_call(
        _gdn_gate_kernel,
        out_shape=jax.ShapeDtypeStruct((bsz, r, t_len), F32),
        grid=(bsz, t_len // tm),
        in_specs=[pl.BlockSpec((1, tm, d), lambda bi, i: (bi, i, 0)),
                  pl.BlockSpec((1, N_MOD, d), (lambda bi, i: (bi, 0, 0)) if per_batch else (lambda bi, i: (0, 0, 0))),
                  pl.BlockSpec((r, d), small),
                  pl.BlockSpec((r, 1), small),
                  pl.BlockSpec((r, 1), small),
                  pl.BlockSpec((r, 1), small)],
        out_specs=pl.BlockSpec((1, r, tm), lambda bi, i: (bi, 0, i)),
        compiler_params=_cparams("arbitrary", "arbitrary"),
        name="gdn_gates",
    )(x, mod, wt, bt, alog, dtb)


CONV_ROWS = 256
CONV_HALO = SUBLANES


def _gdn_conv_kernel(x_ref, w_ref, o_ref, pad_scr):
    t_len = x_ref.shape[1]
    j = pl.program_id(1)
    zeros = jnp.zeros((CONV_HALO, LANES), F32)
    pad_scr[0:CONV_HALO, :] = zeros
    pad_scr[t_len + CONV_HALO:t_len + 2 * CONV_HALO, :] = zeros
    pad_scr[CONV_HALO:t_len + CONV_HALO, :] = x_ref[0]
    heads_qk = 2 * GDN_HEADS
    is_v = j >= heads_qk
    qk_scale = jnp.where(j < GDN_HEADS, GDN_DK ** -0.5, 1.0).astype(F32)
    w = w_ref[...]
    half = GDN_CONV // 2

    def body(c, carry):
        r0 = pl.multiple_of(c * CONV_ROWS, CONV_ROWS)
        blk = pad_scr[pl.ds(r0, CONV_ROWS + 2 * CONV_HALO), :]
        acc = jnp.zeros((CONV_ROWS, LANES), F32)
        for k in range(GDN_CONV):
            off = CONV_HALO - half + k
            acc = acc + blk[off:off + CONV_ROWS, :] * w[k:k + 1, :]
        y = acc * jax.nn.sigmoid(acc)
        ss = jnp.sum(y * y, axis=-1, keepdims=True)
        yn = y * lax.rsqrt(ss + RMS_EPS) * qk_scale
        o_ref[0, pl.ds(r0, CONV_ROWS), :] = jnp.where(is_v, y, yn).astype(o_ref.dtype)
        return carry

    lax.fori_loop(0, t_len // CONV_ROWS, body, 0)


def _gdn_conv(p, conv_w, col0):
    bsz, t_len, _ = p.shape
    nblk = GDN_QKV_W // LANES
    cb = col0 // LANES
    return pl.pallas_call(
        _gdn_conv_kernel,
        out_shape=jax.ShapeDtypeStruct((bsz, t_len, GDN_QKV_W), BF16),
        grid=(bsz, nblk),
        in_specs=[pl.BlockSpec((1, t_len, LANES), lambda bi, j: (bi, 0, cb + j)),
                  pl.BlockSpec((GDN_CONV, LANES), lambda bi, j: (0, j))],
        out_specs=pl.BlockSpec((1, t_len, LANES), lambda bi, j: (bi, 0, j)),
        scratch_shapes=[pltpu.VMEM((t_len + 2 * CONV_HALO, LANES), F32)],
        compiler_params=_cparams("arbitrary", "arbitrary"),
        name="gdn_conv",
    )(p, conv_w)


def _row_to_col(row, eye):
    return jnp.sum(jnp.where(eye, jnp.broadcast_to(row, eye.shape), 0.0), axis=1, keepdims=True)


def _col_to_row(col, eye):
    return jnp.sum(jnp.where(eye, jnp.broadcast_to(col, eye.shape), 0.0), axis=0, keepdims=True)


def _gdn_chunk(q, k, v, g_row, b_row, s, rev):
    c_len = q.shape[0]
    ii = lax.broadcasted_iota(jnp.int32, (c_len, c_len), 0)
    jj = lax.broadcasted_iota(jnp.int32, (c_len, c_len), 1)
    eye = ii == jj
    if rev:
        incl = jj >= ii
        strict = jj > ii
    else:
        incl = jj <= ii
        strict = jj < ii
    g_b = jnp.broadcast_to(g_row, (c_len, c_len))
    cum_col = jnp.sum(jnp.where(incl, g_b, 0.0), axis=1, keepdims=True)
    cum_row = _col_to_row(cum_col, eye)
    beta_col = _row_to_col(b_row, eye)
    g_last = jnp.sum(g_row, axis=1, keepdims=True)
    decay = jnp.where(incl, jnp.exp(jnp.where(incl, cum_col - cum_row, 0.0)), 0.0)

    qf = q.astype(F32)
    kf = k.astype(F32)
    vf = v.astype(F32)
    k_beta = kf * beta_col
    neg_m = jnp.where(strict, -(_dot_nt(k_beta.astype(BF16), k) * decay), 0.0)

    shift0 = int(math.log2(GDN_SERIES_BLOCK))
    same0 = (ii >> shift0) == (jj >> shift0)
    n_d = jnp.where(same0, neg_m, 0.0)
    inv = jnp.where(eye, 1.0, 0.0) + n_d
    power = n_d
    for _ in range(shift0 - 1):
        pb = power.astype(BF16)
        power = _dot(pb, pb)
        inv = inv + _dot(inv.astype(BF16), power.astype(BF16))
    for sh in range(shift0, int(math.log2(c_len))):
        off = jnp.where(((ii >> (sh + 1)) == (jj >> (sh + 1))) & ((ii >> sh) != (jj >> sh)), neg_m, 0.0)
        inv_b = inv.astype(BF16)
        inv = inv + _dot(_dot(inv_b, off.astype(BF16)).astype(BF16), inv_b)
    inv_b = inv.astype(BF16)

    e_col = jnp.exp(cum_col)
    u = _dot(inv_b, (vf * beta_col).astype(BF16))
    w = _dot(inv_b, (k_beta * e_col).astype(BF16))
    a_qk = jnp.where(incl, _dot_nt(q, k) * decay, 0.0)
    q_dec = (qf * e_col).astype(BF16)
    k_dec = kf * jnp.exp(g_last - cum_col)
    s_b = s.astype(BF16)
    v_new = u - _dot(w.astype(BF16), s_b)
    v_new_b = v_new.astype(BF16)
    o = _dot(q_dec, s_b) + _dot(a_qk.astype(BF16), v_new_b)
    s_new = s * jnp.exp(g_last) + _dot(k_dec.T.astype(BF16), v_new_b)
    return o, s_new


GATED_NORM_ROWS = 512


def _gdn_kernel(ql_ref, kl_ref, vl_ref, qc_ref, kc_ref, vc_ref,
                bfl_ref, bbl_ref, gfl_ref, gbl_ref, bfc_ref, bbc_ref, gfc_ref, gbc_ref,
                zl_ref, zc_ref, nw_ref, yl_ref, yc_ref, ol_scr, oc_scr, sf_scr, sb_scr):
    c_len = GDN_CHUNK
    t_len = ql_ref.shape[1]
    l_len = qc_ref.shape[1]

    ol_scr[...] = jnp.zeros_like(ol_scr)
    oc_scr[...] = jnp.zeros_like(oc_scr)
    sf_scr[...] = jnp.zeros_like(sf_scr)
    sb_scr[...] = jnp.zeros_like(sb_scr)

    def run(q_ref, k_ref, v_ref, bf_ref, bb_ref, gf_ref, gb_ref, o_scr, n_chunk):
        def body(n, carry):
            for rev, b_ref, g_ref, s_scr in ((False, bf_ref, gf_ref, sf_scr), (True, bb_ref, gb_ref, sb_scr)):
                idx = (n_chunk - 1 - n) if rev else n
                r0 = pl.multiple_of(idx * c_len, c_len)
                o, s_new = _gdn_chunk(q_ref[0, pl.ds(r0, c_len), :], k_ref[0, pl.ds(r0, c_len), :],
                                      v_ref[0, pl.ds(r0, c_len), :],
                                      g_ref[0, 0, pl.ds(idx, 1), :], b_ref[0, 0, pl.ds(idx, 1), :],
                                      s_scr[...], rev)
                s_scr[...] = s_new
                o_scr[pl.ds(r0, c_len), :] += o
            return carry

        lax.fori_loop(0, n_chunk, body, 0)

    run(qc_ref, kc_ref, vc_ref, bfc_ref, bbc_ref, gfc_ref, gbc_ref, oc_scr, l_len // c_len)
    run(ql_ref, kl_ref, vl_ref, bfl_ref, bbl_ref, gfl_ref, gbl_ref, ol_scr, t_len // c_len)

    nw = nw_ref[...]

    def gated_norm(o_scr, z_ref, y_ref, n_rows):
        rows = min(GATED_NORM_ROWS, n_rows)

        def body(c, carry):
            r0 = pl.multiple_of(c * rows, rows)
            o = o_scr[pl.ds(r0, rows), :]
            z = z_ref[0, pl.ds(r0, rows), :]
            on = o * lax.rsqrt(jnp.mean(o * o, axis=-1, keepdims=True) + RMS_EPS) * nw
            y_ref[0, pl.ds(r0, rows), :] = (on * (z * jax.nn.sigmoid(z))).astype(y_ref.dtype)
            return carry

        lax.fori_loop(0, n_rows // rows, body, 0)

    gated_norm(ol_scr, zl_ref, yl_ref, t_len)
    gated_norm(oc_scr, zc_ref, yc_ref, l_len)


def _gdn(qkv_l, qkv_c, gates_l, gates_c, p_l, p_c, z_col0, norm_w):
    bsz, t_len, _ = qkv_l.shape
    l_len = qkv_c.shape[1]
    c_len = GDN_CHUNK
    nh = GDN_HEADS
    gl = gates_l.reshape(bsz, 4 * nh, t_len // c_len, c_len)
    gc = gates_c.reshape(bsz, 4 * nh, l_len // c_len, c_len)
    zb = z_col0 // LANES

    def seq_spec(n_rows, col_off):
        return pl.BlockSpec((1, n_rows, LANES), lambda bi, h: (bi, 0, col_off + h))

    def gate_spec(n_chunk, row_off):
        return pl.BlockSpec((1, 1, n_chunk, c_len), lambda bi, h: (bi, row_off + h, 0, 0))

    in_specs = ([seq_spec(t_len, 0), seq_spec(t_len, nh), seq_spec(t_len, 2 * nh),
                 seq_spec(l_len, 0), seq_spec(l_len, nh), seq_spec(l_len, 2 * nh)]
                + [gate_spec(t_len // c_len, r * nh) for r in range(4)]
                + [gate_spec(l_len // c_len, r * nh) for r in range(4)]
                + [seq_spec(t_len, zb), seq_spec(l_len, zb),
                   pl.BlockSpec((1, LANES), lambda bi, h: (0, 0))])
    return pl.pallas_call(
        _gdn_kernel,
        out_shape=(jax.ShapeDtypeStruct((bsz, t_len, GDN_V_W), BF16),
                   jax.ShapeDtypeStruct((bsz, l_len, GDN_V_W), BF16)),
        grid=(bsz, nh),
        in_specs=in_specs,
        out_specs=(seq_spec(t_len, 0), seq_spec(l_len, 0)),
        scratch_shapes=[pltpu.VMEM((t_len, GDN_DV), F32), pltpu.VMEM((l_len, GDN_DV), F32),
                        pltpu.VMEM((GDN_DK, GDN_DV), F32), pltpu.VMEM((GDN_DK, GDN_DV), F32)],
        compiler_params=_cparams("arbitrary", "arbitrary"),
        name="gdn_scan",
    )(qkv_l, qkv_l, qkv_l, qkv_c, qkv_c, qkv_c, gl, gl, gl, gl, gc, gc, gc, gc, p_l, p_c, norm_w)


def _swap_halves(x):
    return pltpu.roll(x, LANES // 2, axis=1) if x.shape[1] == LANES else jnp.concatenate(
        [_swap_halves(x[:, c * LANES:(c + 1) * LANES]) for c in range(x.shape[1] // LANES)], axis=1)


def _dup_groups(x):
    lane = lax.broadcasted_iota(jnp.int32, (x.shape[0], LANES), 1)
    lo = lane < LANES // 2
    outs = []
    for c in range(x.shape[1] // LANES):
        xc = x[:, c * LANES:(c + 1) * LANES]
        xs = pltpu.roll(xc, LANES // 2, axis=1)
        outs.append(jnp.where(lo, xc, xs))
        outs.append(jnp.where(lo, xs, xc))
    return jnp.concatenate(outs, axis=1)


def _rope_kernel(q_ref, k_ref, v_ref, cos_ref, sin_ref, qo_ref, ko_ref, vo_ref, *, rope):
    q = q_ref[0]
    k = k_ref[0]
    if rope:
        cos = cos_ref[...]
        sin = sin_ref[...]
        quarter = SWA_HEAD_DIM // 2
        lane = lax.broadcasted_iota(jnp.int32, cos.shape, 1)
        first = (lane % SWA_HEAD_DIM) < quarter

        def rot(x):
            outs = []
            for c in range(x.shape[1] // LANES):
                xc = x[:, c * LANES:(c + 1) * LANES]
                partner = jnp.where(first, pltpu.roll(xc, LANES - quarter, axis=1), pltpu.roll(xc, quarter, axis=1))
                outs.append(xc * cos + partner * sin)
            return jnp.concatenate(outs, axis=1)

        q = rot(q)
        k = rot(k)
    qo_ref[0] = (q * (SWA_HEAD_DIM ** -0.5)).astype(qo_ref.dtype)
    ko_ref[0] = _dup_groups(k).astype(ko_ref.dtype)
    vo_ref[0] = _dup_groups(v_ref[0]).astype(vo_ref.dtype)


def _swa_prep(p, q_col0, k_col0, v_col0, cos_t, sin_t, rope):
    bsz, t_len, _ = p.shape
    tr = _pick_tile(t_len, 512)
    return pl.pallas_call(
        functools.partial(_rope_kernel, rope=rope),
        out_shape=(jax.ShapeDtypeStruct((bsz, t_len, SWA_Q_W), BF16),
                   jax.ShapeDtypeStruct((bsz, t_len, 2 * SWA_KV_W), BF16),
                   jax.ShapeDtypeStruct((bsz, t_len, 2 * SWA_KV_W), BF16)),
        grid=(bsz, t_len // tr),
        in_specs=[pl.BlockSpec((1, tr, SWA_Q_W), lambda bi, i: (bi, i, q_col0 // SWA_Q_W)),
                  pl.BlockSpec((1, tr, SWA_KV_W), lambda bi, i: (bi, i, k_col0 // SWA_KV_W)),
                  pl.BlockSpec((1, tr, SWA_KV_W), lambda bi, i: (bi, i, v_col0 // SWA_KV_W)),
                  pl.BlockSpec((tr, LANES), lambda bi, i: (i, 0)),
                  pl.BlockSpec((tr, LANES), lambda bi, i: (i, 0))],
        out_specs=(pl.BlockSpec((1, tr, SWA_Q_W), lambda bi, i: (bi, i, 0)),
                   pl.BlockSpec((1, tr, 2 * SWA_KV_W), lambda bi, i: (bi, i, 0)),
                   pl.BlockSpec((1, tr, 2 * SWA_KV_W), lambda bi, i: (bi, i, 0))),
        compiler_params=_cparams("arbitrary", "arbitrary"),
        name="swa_prep",
    )(p, p, p, cos_t, sin_t)


def _rope_tables(t_len):
    rows = t_len // GRID_W
    row = jnp.broadcast_to(jnp.arange(rows)[:, None], (rows, GRID_W)).reshape(t_len).astype(F32)
    col = jnp.broadcast_to(jnp.arange(GRID_W)[None, :], (rows, GRID_W)).reshape(t_len).astype(F32)
    n_freq = SWA_HEAD_DIM // 4
    freq = jnp.power(ROPE_THETA, -jnp.arange(n_freq, dtype=F32) / n_freq)
    ang = jnp.concatenate([row[:, None] * freq, col[:, None] * freq], axis=-1)
    cos = jnp.cos(ang)
    sin = jnp.sin(ang)
    reps = LANES // SWA_HEAD_DIM
    return (jnp.tile(jnp.concatenate([cos, cos], axis=-1), (1, reps)),
            jnp.tile(jnp.concatenate([-sin, sin], axis=-1), (1, reps)))


def _attend(q, key_blocks, sink_ref, o_ref):
    n_rows = q.shape[0]
    lane = lax.broadcasted_iota(jnp.int32, (n_rows, LANES), 1)
    lo = lane < LANES // 2
    heads_per_group = SWA_Q_HEADS // SWA_KV_HEADS
    for c in range(SWA_Q_W // LANES):
        qc = q[:, c * LANES:(c + 1) * LANES]
        g = (2 * c) // heads_per_group
        halves = []
        for half in range(2):
            head = 2 * c + half
            qm = jnp.where(lo if half == 0 else jnp.logical_not(lo), qc, jnp.zeros_like(qc))
            sink = sink_ref[0:1, head:head + 1]
            scores = []
            m = jnp.broadcast_to(sink, (n_rows, 1))
            for k2, _, mask in key_blocks:
                s = _dot_nt(qm, k2[:, g * LANES:(g + 1) * LANES])
                if mask is not None:
                    s = jnp.where(mask, s, NEG_BIG)
                scores.append(s)
                m = jnp.maximum(m, jnp.max(s, axis=-1, keepdims=True))
            denom = jnp.exp(sink - m)
            acc = jnp.zeros((n_rows, LANES), F32)
            for s, (_, v2, _) in zip(scores, key_blocks):
                pr = jnp.exp(s - m)
                denom = denom + jnp.sum(pr, axis=-1, keepdims=True)
                acc = acc + _dot(pr.astype(BF16), v2[:, g * LANES:(g + 1) * LANES])
            halves.append(acc * (1.0 / denom))
        o_ref[0, :, c * LANES:(c + 1) * LANES] = jnp.where(lo, halves[0], halves[1]).astype(o_ref.dtype)


def _swa_latent_kernel(q_ref, kp_ref, kc_ref, kn_ref, vp_ref, vc_ref, vn_ref, kx_ref, vx_ref, sink_ref, o_ref):
    i = pl.program_id(1)
    nblk = pl.num_programs(1)
    blk = q_ref.shape[1]
    qi = lax.broadcasted_iota(jnp.int32, (blk, blk), 0)
    kj = lax.broadcasted_iota(jnp.int32, (blk, blk), 1)
    mask_prev = kj >= qi + jnp.where(i > 0, 0, blk)
    mask_next = kj <= qi - jnp.where(i < nblk - 1, 0, blk)
    blocks = [(kp_ref[0], vp_ref[0], mask_prev), (kc_ref[0], vc_ref[0], None),
              (kn_ref[0], vn_ref[0], mask_next), (kx_ref[0], vx_ref[0], None)]
    _attend(q_ref[0], blocks, sink_ref, o_ref)


def _swa_latent(q, k2, v2, k2c, v2c, sinks):
    bsz, t_len, _ = q.shape
    l_len = k2c.shape[1]
    blk = SWA_BLOCK
    nblk = t_len // blk
    kvw = k2.shape[2]
    prev = pl.BlockSpec((1, blk, kvw), lambda bi, i: (bi, jnp.maximum(i - 1, 0), 0))
    cur = pl.BlockSpec((1, blk, kvw), lambda bi, i: (bi, i, 0))
    nxt = pl.BlockSpec((1, blk, kvw), lambda bi, i: (bi, jnp.minimum(i + 1, nblk - 1), 0))
    ctx = pl.BlockSpec((1, l_len, kvw), lambda bi, i: (bi, 0, 0))
    return pl.pallas_call(
        _swa_latent_kernel,
        out_shape=jax.ShapeDtypeStruct((bsz, t_len, SWA_Q_W), BF16),
        grid=(bsz, nblk),
        in_specs=[pl.BlockSpec((1, blk, SWA_Q_W), lambda bi, i: (bi, i, 0)),
                  prev, cur, nxt, prev, cur, nxt, ctx, ctx,
                  pl.BlockSpec((1, SWA_Q_HEADS), lambda bi, i: (0, 0))],
        out_specs=pl.BlockSpec((1, blk, SWA_Q_W), lambda bi, i: (bi, i, 0)),
        compiler_params=_cparams("arbitrary", "arbitrary"),
        name="swa_latent",
    )(q, k2, k2, k2, v2, v2, v2, k2c, v2c, sinks)


def _swa_context_kernel(q_ref, k_ref, v_ref, sink_ref, o_ref):
    _attend(q_ref[0], [(k_ref[0], v_ref[0], None)], sink_ref, o_ref)


def _swa_context(q, k2, v2, sinks):
    bsz, l_len, _ = q.shape
    kvw = k2.shape[2]
    return pl.pallas_call(
        _swa_context_kernel,
        out_shape=jax.ShapeDtypeStruct((bsz, l_len, SWA_Q_W), BF16),
        grid=(bsz,),
        in_specs=[pl.BlockSpec((1, l_len, SWA_Q_W), lambda bi: (bi, 0, 0)),
                  pl.BlockSpec((1, l_len, kvw), lambda bi: (bi, 0, 0)),
                  pl.BlockSpec((1, l_len, kvw), lambda bi: (bi, 0, 0)),
                  pl.BlockSpec((1, SWA_Q_HEADS), lambda bi: (0, 0))],
        out_specs=pl.BlockSpec((1, l_len, SWA_Q_W), lambda bi: (bi, 0, 0)),
        compiler_params=_cparams("arbitrary"),
        name="swa_context",
    )(q, k2, v2, sinks)


def _sgu_kernel(u_ref, v_ref, g_ref, b_ref, ws_ref, bs_ref, o_ref):
    n_rows = u_ref.shape[1]
    v = _layer_norm_rows(_gelu_tanh(v_ref[0]), g_ref[...], b_ref[...]).astype(BF16)
    cw = v.shape[1] // SGU_GROUPS
    for c in range(n_rows // SGU_CHUNK):
        rs = slice(c * SGU_CHUNK, (c + 1) * SGU_CHUNK)
        for g in range(SGU_GROUPS):
            cs = slice(g * cw, (g + 1) * cw)
            s = _dot(ws_ref[g], v[rs, cs]) + bs_ref[:, g:g + 1]
            o_ref[0, rs, cs] = (_gelu_tanh(u_ref[0, rs, cs]) * s).astype(o_ref.dtype)


def _sgu(p, u_col0, v_col0, ln_g, ln_b, ws, bs_t):
    bsz, t_len, _ = p.shape
    w = ln_g.shape[1]
    tr = _pick_tile(t_len, 2 * SGU_CHUNK)
    full = lambda bi, i: (0, 0)
    return pl.pallas_call(
        _sgu_kernel,
        out_shape=jax.ShapeDtypeStruct((bsz, t_len, w), BF16),
        grid=(bsz, t_len // tr),
        in_specs=[pl.BlockSpec((1, tr, w), lambda bi, i: (bi, i, u_col0 // w)),
                  pl.BlockSpec((1, tr, w), lambda bi, i: (bi, i, v_col0 // w)),
                  pl.BlockSpec((1, w), full), pl.BlockSpec((1, w), full),
                  pl.BlockSpec(ws.shape, lambda bi, i: (0, 0, 0)),
                  pl.BlockSpec(bs_t.shape, full)],
        out_specs=pl.BlockSpec((1, tr, w), lambda bi, i: (bi, i, 0)),
        compiler_params=_cparams("arbitrary", "arbitrary"),
        name="sgu",
    )(p, p, ln_g, ln_b, ws, bs_t)


def _merge_kernel(ya_ref, yb_ref, yc_ref, ga_ref, gb_ref, gc_ref, wa_ref, wb_ref, wc_ref, o_ref):
    y = (jax.nn.sigmoid(ga_ref[0]) * _dot(ya_ref[0], wa_ref[...])
         + jax.nn.sigmoid(gb_ref[0]) * _dot(yb_ref[0], wb_ref[...])
         + jax.nn.sigmoid(gc_ref[0]) * _dot(yc_ref[0], wc_ref[...]))
    o_ref[0] = y.astype(o_ref.dtype)


def _merge(ya, yb, yc, p, gate_col0, wa, wb, wc):
    bsz, t_len, _ = ya.shape
    d = wa.shape[1]
    tm = _pick_tile(t_len, 1024)
    tn = _pick_tile(d, 512)
    nb = d // tn
    g0 = gate_col0 // tn

    def yspec(width):
        return pl.BlockSpec((1, tm, width), lambda bi, i, j: (bi, i, 0))

    def gspec(k):
        return pl.BlockSpec((1, tm, tn), lambda bi, i, j: (bi, i, g0 + k * nb + j))

    def wspec(width):
        return pl.BlockSpec((width, tn), lambda bi, i, j: (0, j))

    return pl.pallas_call(
        _merge_kernel,
        out_shape=jax.ShapeDtypeStruct((bsz, t_len, d), BF16),
        grid=(bsz, t_len // tm, nb),
        in_specs=[yspec(ya.shape[2]), yspec(yb.shape[2]), yspec(yc.shape[2]),
                  gspec(0), gspec(1), gspec(2),
                  wspec(wa.shape[0]), wspec(wb.shape[0]), wspec(wc.shape[0])],
        out_specs=pl.BlockSpec((1, tm, tn), lambda bi, i, j: (bi, i, j)),
        compiler_params=_cparams("arbitrary", "arbitrary", "arbitrary"),
        name="merge",
    )(ya, yb, yc, p, p, p, wa, wb, wc)


def _outproj_kernel(y_ref, w_ref, x_ref, mod_ref, g_ref, b_ref, o_ref, *, alpha):
    o = _dot(y_ref[0], w_ref[...])
    r = alpha * x_ref[0] + mod_ref[0, 2:3, :] * o
    o_ref[0] = _layer_norm_rows(r, g_ref[...], b_ref[...])


def _outproj(y, w, x, mod, ln_g, ln_b, alpha):
    bsz, t_len, d = x.shape
    tm = _pick_tile(t_len, 512)
    per_batch = mod.shape[0] > 1
    full = lambda bi, i: (0, 0)
    return pl.pallas_call(
        functools.partial(_outproj_kernel, alpha=alpha),
        out_shape=jax.ShapeDtypeStruct((bsz, t_len, d), F32),
        grid=(bsz, t_len // tm),
        in_specs=[pl.BlockSpec((1, tm, d), lambda bi, i: (bi, i, 0)),
                  pl.BlockSpec((d, d), full),
                  pl.BlockSpec((1, tm, d), lambda bi, i: (bi, i, 0)),
                  pl.BlockSpec((1, N_MOD, d), (lambda bi, i: (bi, 0, 0)) if per_batch else (lambda bi, i: (0, 0, 0))),
                  pl.BlockSpec((1, d), full), pl.BlockSpec((1, d), full)],
        out_specs=pl.BlockSpec((1, tm, d), lambda bi, i: (bi, i, 0)),
        compiler_params=_cparams("arbitrary", "arbitrary"),
        name="out_proj_ln",
    )(y, w, x, mod, ln_g, ln_b)


def _ffn_kernel(x_ref, mod_ref, w1_ref, w2_ref, g_ref, b_ref, o_ref, h_scr, acc_scr, *, alpha):
    j = pl.program_id(2)

    @pl.when(j == 0)
    def _():
        h_scr[...] = (x_ref[0] * (1.0 + mod_ref[0, 4:5, :]) + mod_ref[0, 3:4, :]).astype(BF16)
        acc_scr[...] = jnp.zeros_like(acc_scr)

    a = jnp.maximum(_dot(h_scr[...], w1_ref[...]), 0.0)
    acc_scr[...] += _dot((a * a).astype(BF16), w2_ref[...])

    @pl.when(j == pl.num_programs(2) - 1)
    def _():
        r = alpha * x_ref[0] + mod_ref[0, 5:6, :] * acc_scr[...]
        o_ref[0] = _layer_norm_rows(r, g_ref[...], b_ref[...])


def _ffn(x, mod, w1, w2, ln_g, ln_b, alpha):
    bsz, t_len, d = x.shape
    dff = w1.shape[1]
    tm = _pick_tile(t_len, 512)
    tf = _pick_tile(dff, 1024)
    per_batch = mod.shape[0] > 1
    full = lambda bi, i, j: (0, 0)
    return pl.pallas_call(
        functools.partial(_ffn_kernel, alpha=alpha),
        out_shape=jax.ShapeDtypeStruct((bsz, t_len, d), F32),
        grid=(bsz, t_len // tm, dff // tf),
        in_specs=[pl.BlockSpec((1, tm, d), lambda bi, i, j: (bi, i, 0)),
                  pl.BlockSpec((1, N_MOD, d), (lambda bi, i, j: (bi, 0, 0)) if per_batch else (lambda bi, i, j: (0, 0, 0))),
                  pl.BlockSpec((d, tf), lambda bi, i, j: (0, j)),
                  pl.BlockSpec((tf, d), lambda bi, i, j: (j, 0)),
                  pl.BlockSpec((1, d), full), pl.BlockSpec((1, d), full)],
        out_specs=pl.BlockSpec((1, tm, d), lambda bi, i, j: (bi, i, 0)),
        scratch_shapes=[pltpu.VMEM((tm, d), BF16), pltpu.VMEM((tm, d), F32)],
        compiler_params=_cparams("arbitrary", "arbitrary", "arbitrary"),
        name="ffn_ln",
    )(x, mod, w1, w2, ln_g, ln_b)


def _in_proj_layout(d_model):
    sgu_w = d_model // 2
    src = {}
    off = 0
    for name, width in (("qkv", GDN_QKV_W), ("z", GDN_V_W), ("gdn_gates", 4 * GDN_HEADS),
                        ("swa_q", SWA_Q_W), ("swa_k", SWA_KV_W), ("swa_v", SWA_KV_W),
                        ("sgu_u", sgu_w), ("sgu_v", sgu_w),
                        ("gate_a", d_model), ("gate_b", d_model), ("gate_c", d_model)):
        src[name] = (off, width)
        off += width
    order = ("gate_a", "gate_b", "gate_c", "z", "swa_q", "sgu_u", "sgu_v", "qkv", "swa_k", "swa_v")
    dst = {}
    off = 0
    for name in order:
        dst[name] = off
        assert off % min(src[name][1], 2048) == 0 or name == "qkv"
        off += src[name][1]
    return src, order, dst


def kernel(x, c, ctx, c_ctx, w_ada, b_ada, w_in, b_in, gdn_conv, gdn_a_log, gdn_dt_bias, gdn_norm, swa_sinks, sgu_ln_g, sgu_ln_b, sgu_w, sgu_b, w_branch_a, w_branch_b, w_branch_c, w_out, ln_mix_g, ln_mix_b, w_ff1, w_ff2, ln_ff_g, ln_ff_b):
    bsz, t_len, d = x.shape
    depth = w_ada.shape[0]
    alpha = (2 * depth) ** 0.25
    src, order, dst = _in_proj_layout(d)

    pad_rows = (-(bsz + 1)) % SUBLANES
    cc = jnp.concatenate([c, c_ctx[None, :], jnp.zeros((pad_rows, d), F32)], axis=0)
    mod_all = _ada_mod(cc, w_ada, b_ada).reshape(depth, bsz + 1 + pad_rows, N_MOD, d)
    cos_t, sin_t = _rope_tables(t_len)
    ones_t = jnp.ones((ctx.shape[1], LANES), F32)
    zeros_t = jnp.zeros((ctx.shape[1], LANES), F32)

    xc = ctx
    for l in range(depth):
        need_ctx = l < depth - 1
        mod_l = mod_all[l, :bsz]
        mod_c = mod_all[l, bsz:bsz + 1]

        def cols(a, name):
            s0, wd = src[name]
            return a[..., s0:s0 + wd]

        w_main = jnp.concatenate([cols(w_in[l], n) for n in order], axis=1).astype(BF16)
        b_main = jnp.concatenate([cols(b_in[l], n) for n in order], axis=0)[None, :]
        wt_g = cols(w_in[l], "gdn_gates").T.astype(BF16)
        bt_g = cols(b_in[l], "gdn_gates")[:, None]
        nh2 = 2 * GDN_HEADS
        alog = jnp.concatenate([jnp.zeros((nh2,), F32), gdn_a_log[l].reshape(nh2)])[:, None]
        dtb = jnp.concatenate([jnp.zeros((nh2,), F32), gdn_dt_bias[l].reshape(nh2)])[:, None]

        p_l = _inproj(x, mod_l, w_main, b_main)
        p_c = _inproj(xc, mod_c, w_main, b_main)
        gates_l = _gdn_gates(x, mod_l, wt_g, bt_g, alog, dtb)
        gates_c = _gdn_gates(xc, mod_c, wt_g, bt_g, alog, dtb)

        qkv_l = _gdn_conv(p_l, gdn_conv[l], dst["qkv"])
        qkv_c = _gdn_conv(p_c, gdn_conv[l], dst["qkv"])
        ya_l, ya_c = _gdn(qkv_l, qkv_c, gates_l, gates_c, p_l, p_c, dst["z"], gdn_norm[l][None, :])

        sinks = swa_sinks[l][None, :]
        q_l, k2_l, v2_l = _swa_prep(p_l, dst["swa_q"], dst["swa_k"], dst["swa_v"], cos_t, sin_t, True)
        q_c, k2_c, v2_c = _swa_prep(p_c, dst["swa_q"], dst["swa_k"], dst["swa_v"], ones_t, zeros_t, False)
        yb_l = _swa_latent(q_l, k2_l, v2_l, k2_c, v2_c, sinks)

        ws = sgu_w[l].astype(BF16)
        bs_t = sgu_b[l].T
        sg, sb = sgu_ln_g[l][None, :], sgu_ln_b[l][None, :]
        yc_l = _sgu(p_l, dst["sgu_u"], dst["sgu_v"], sg, sb, ws, bs_t)

        wa, wb, wc = w_branch_a[l].astype(BF16), w_branch_b[l].astype(BF16), w_branch_c[l].astype(BF16)
        wo = w_out[l].astype(BF16)
        w1, w2 = w_ff1[l].astype(BF16), w_ff2[l].astype(BF16)
        lmg, lmb = ln_mix_g[l][None, :], ln_mix_b[l][None, :]
        lfg, lfb = ln_ff_g[l][None, :], ln_ff_b[l][None, :]

        y_l = _merge(ya_l, yb_l, yc_l, p_l, dst["gate_a"], wa, wb, wc)
        x = _outproj(y_l, wo, x, mod_l, lmg, lmb, alpha)
        x = _ffn(x, mod_l, w1, w2, lfg, lfb, alpha)
        if need_ctx:
            yb_c = _swa_context(q_c, k2_c, v2_c, sinks)
            yc_c = _sgu(p_c, dst["sgu_u"], dst["sgu_v"], sg, sb, ws, bs_t)
            y_c = _merge(ya_c, yb_c, yc_c, p_c, dst["gate_a"], wa, wb, wc)
            xc = _outproj(y_c, wo, xc, mod_c, lmg, lmb, alpha)
            xc = _ffn(xc, mod_c, w1, w2, lfg, lfb, alpha)
    return x
```

```python
import functools
import math

import jax
import jax.numpy as jnp
import numpy as np
from jax import lax
from jax.experimental import pallas as pl
from jax.experimental.pallas import tpu as pltpu

F32 = jnp.float32
BF16 = jnp.bfloat16

GRID_W = 64
GDN_HEADS = 8
GDN_DK = 128
GDN_DV = 128
GDN_CONV = 5
SWA_Q_HEADS = 16
SWA_KV_HEADS = 4
SWA_HEAD_DIM = 64
SWA_BLOCK = 128
ROPE_THETA = 10000.0
SGU_GROUPS = 8
SGU_CHUNK = 128
N_MOD = 6
LN_EPS = 1e-5
RMS_EPS = 1e-6

GDN_QK_W = GDN_HEADS * GDN_DK
GDN_V_W = GDN_HEADS * GDN_DV
GDN_QKV_W = 2 * GDN_QK_W + GDN_V_W
SWA_Q_W = SWA_Q_HEADS * SWA_HEAD_DIM
SWA_KV_W = SWA_KV_HEADS * SWA_HEAD_DIM

LANES = 128
SUBLANES = 8
VMEM_LIMIT_BYTES = 56 * 1024 * 1024

GDN_CHUNK = 128
GDN_SERIES_BLOCK = 16
assert GDN_CHUNK == GDN_DK == GDN_DV == LANES
NEG_BIG = -1e30


def _cparams(*sem):
    return pltpu.CompilerParams(dimension_semantics=sem, vmem_limit_bytes=VMEM_LIMIT_BYTES)


def _dot(a, b):
    return jnp.dot(a, b, preferred_element_type=F32)


def _dot_nt(a, b):
    return lax.dot_general(a, b, (((1,), (1,)), ((), ())), preferred_element_type=F32)


def _layer_norm_rows(r, g, b):
    mu = jnp.mean(r, axis=-1, keepdims=True)
    d = r - mu
    var = jnp.mean(d * d, axis=-1, keepdims=True)
    return d * lax.rsqrt(var + LN_EPS) * g + b


def _gelu_tanh(x):
    return x * (0.5 * (1.0 + jnp.tanh(math.sqrt(2.0 / math.pi) * (x + 0.044715 * (x * x * x)))))


def _pick_tile(n, pref):
    t = min(n, pref)
    while n % t:
        t //= 2
    return t


def _ada_kernel(c_ref, w_ref, b_ref, o_ref):
    c = c_ref[...]
    a = (c * jax.nn.sigmoid(c)).astype(BF16)
    o_ref[0] = _dot(a, w_ref[0].astype(BF16)) + b_ref[0]


def _ada_mod(cc, w_ada, b_ada):
    nl, d, n = w_ada.shape
    tn = _pick_tile(n, 1024)
    return pl.pallas_call(
        _ada_kernel,
        out_shape=jax.ShapeDtypeStruct((nl, cc.shape[0], n), F32),
        grid=(nl, n // tn),
        in_specs=[pl.BlockSpec(cc.shape, lambda l, j: (0, 0)),
                  pl.BlockSpec((1, d, tn), lambda l, j: (l, 0, j)),
                  pl.BlockSpec((1, 1, tn), lambda l, j: (l, 0, j))],
        out_specs=pl.BlockSpec((1, cc.shape[0], tn), lambda l, j: (l, 0, j)),
        compiler_params=_cparams("arbitrary", "arbitrary"),
        name="ada_mod",
    )(cc, w_ada, b_ada.reshape(nl, 1, n))


def _inproj_kernel(x_ref, mod_ref, w_ref, b_ref, o_ref, h_scr):
    @pl.when(pl.program_id(2) == 0)
    def _():
        shift = mod_ref[0, 0:1, :]
        scale = mod_ref[0, 1:2, :]
        h_scr[...] = (x_ref[0] * (1.0 + scale) + shift).astype(BF16)

    o_ref[0] = _dot(h_scr[...], w_ref[...]) + b_ref[...]


def _inproj(x, mod, w, b):
    bsz, t_len, d = x.shape
    n = w.shape[1]
    tm = _pick_tile(t_len, 512)
    tn = _pick_tile(n, 1536)
    per_batch = mod.shape[0] > 1
    return pl.pallas_call(
        _inproj_kernel,
        out_shape=jax.ShapeDtypeStruct((bsz, t_len, n), F32),
        grid=(bsz, t_len // tm, n // tn),
        in_specs=[pl.BlockSpec((1, tm, d), lambda bi, i, j: (bi, i, 0)),
                  pl.BlockSpec((1, N_MOD, d), (lambda bi, i, j: (bi, 0, 0)) if per_batch else (lambda bi, i, j: (0, 0, 0))),
                  pl.BlockSpec((d, tn), lambda bi, i, j: (0, j)),
                  pl.BlockSpec((1, tn), lambda bi, i, j: (0, j))],
        out_specs=pl.BlockSpec((1, tm, tn), lambda bi, i, j: (bi, i, j)),
        scratch_shapes=[pltpu.VMEM((tm, d), BF16)],
        compiler_params=_cparams("arbitrary", "arbitrary", "arbitrary"),
        name="in_proj",
    )(x, mod, w, b)


GATE_BETA, GATE_CUM, GATE_E, GATE_EK, GATE_ETOT = 0, 16, 32, 48, 64
DIR_STRIDE = GDN_HEADS
GATE_RAW_DECAY = 2 * GDN_HEADS


def _split3_bf16(x):
    hi = x.astype(BF16)
    r1 = x - hi.astype(F32)
    mid = r1.astype(BF16)
    lo = (r1 - mid.astype(F32)).astype(BF16)
    return hi, mid, lo


def _gdn_gate_kernel(x_ref, mod_ref, w_ref, b_ref, alog_ref, dtb_ref, tri_ref, cols_ref, rows_ref):
    c_len = GDN_CHUNK
    tm = x_ref.shape[1]
    shift = mod_ref[0, 0:1, :]
    scale = mod_ref[0, 1:2, :]
    h = (x_ref[0] * (1.0 + scale) + shift).astype(BF16)
    st = _dot(h, w_ref[...]) + b_ref[...]
    lane = lax.broadcasted_iota(jnp.int32, (c_len, LANES), 1)
    is_decay = (lane >= GATE_RAW_DECAY) & (lane < 2 * GATE_RAW_DECAY)
    tri = tri_ref[...]
    for c in range(tm // c_len):
        rs = slice(c * c_len, (c + 1) * c_len)
        stc = st[rs]
        beta = jax.nn.sigmoid(stc)
        xs = stc + dtb_ref[...]
        softplus = jnp.maximum(xs, 0.0) + jnp.log1p(jnp.exp(-jnp.abs(xs)))
        g = jnp.where(is_decay, -jnp.exp(alog_ref[...]) * softplus, 0.0)
        t3 = sum(_dot(tri, piece) for piece in _split3_bf16(g))
        tot = t3[2 * c_len:3 * c_len]
        cum = jnp.where(lane >= GATE_RAW_DECAY + DIR_STRIDE, t3[c_len:2 * c_len], t3[0:c_len])
        e = jnp.exp(cum)
        ek = jnp.exp(tot - cum)
        et = jnp.exp(tot)
        assert GATE_CUM == GATE_RAW_DECAY
        table = jnp.where(lane < GATE_CUM, beta,
                          jnp.where(lane < GATE_E, cum,
                                    jnp.where(lane < GATE_EK, pltpu.roll(e, GATE_E - GATE_RAW_DECAY, axis=1),
                                              jnp.where(lane < GATE_ETOT, pltpu.roll(ek, GATE_EK - GATE_RAW_DECAY, axis=1),
                                                        pltpu.roll(et, GATE_ETOT - GATE_RAW_DECAY, axis=1)))))
        for hh in range(GDN_HEADS):
            cols_ref[0, hh, rs, :] = pltpu.roll(table, LANES - hh, axis=1) if hh else table
        rows_ref[0, :, rs] = cum.T[GATE_RAW_DECAY:2 * GATE_RAW_DECAY]


def _gdn_gates(x, mod, w, b, alog, dtb, tri):
    bsz, t_len, d = x.shape
    tm = _pick_tile(t_len, 512)
    per_batch = mod.shape[0] > 1
    small = lambda bi, i: (0, 0)
    return pl.pallas_call(
        _gdn_gate_kernel,
        out_shape=(jax.ShapeDtypeStruct((bsz, GDN_HEADS, t_len, LANES), F32),
                   jax.ShapeDtypeStruct((bsz, 2 * GDN_HEADS, t_len), F32)),
        grid=(bsz, t_len // tm),
        in_specs=[pl.BlockSpec((1, tm, d), lambda bi, i: (bi, i, 0)),
                  pl.BlockSpec((1, N_MOD, d), (lambda bi, i: (bi, 0, 0)) if per_batch else (lambda bi, i: (0, 0, 0))),
                  pl.BlockSpec((d, LANES), small),
                  pl.BlockSpec((1, LANES), small),
                  pl.BlockSpec((1, LANES), small),
                  pl.BlockSpec((1, LANES), small),
                  pl.BlockSpec(tri.shape, small)],
        out_specs=(pl.BlockSpec((1, GDN_HEADS, tm, LANES), lambda bi, i: (bi, 0, i, 0)),
                   pl.BlockSpec((1, 2 * GDN_HEADS, tm), lambda bi, i: (bi, 0, i))),
        compiler_params=_cparams("arbitrary", "arbitrary"),
        name="gdn_gates",
    )(x, mod, w, b, alog, dtb, tri)


CONV_ROWS = 256
CONV_HALO = SUBLANES


def _gdn_conv_kernel(x_ref, w_ref, o_ref, pad_scr):
    t_len = x_ref.shape[1]
    j = pl.program_id(1)
    zeros = jnp.zeros((CONV_HALO, LANES), F32)
    pad_scr[0:CONV_HALO, :] = zeros
    pad_scr[t_len + CONV_HALO:t_len + 2 * CONV_HALO, :] = zeros
    pad_scr[CONV_HALO:t_len + CONV_HALO, :] = x_ref[0]
    heads_qk = 2 * GDN_HEADS
    is_v = j >= heads_qk
    qk_scale = jnp.where(j < GDN_HEADS, GDN_DK ** -0.5, 1.0).astype(F32)
    w = w_ref[...]
    half = GDN_CONV // 2

    def body(c, carry):
        r0 = pl.multiple_of(c * CONV_ROWS, CONV_ROWS)
        blk = pad_scr[pl.ds(r0, CONV_ROWS + 2 * CONV_HALO), :]
        acc = jnp.zeros((CONV_ROWS, LANES), F32)
        for k in range(GDN_CONV):
            off = CONV_HALO - half + k
            acc = acc + blk[off:off + CONV_ROWS, :] * w[k:k + 1, :]
        y = acc * jax.nn.sigmoid(acc)
        ss = jnp.sum(y * y, axis=-1, keepdims=True)
        yn = y * lax.rsqrt(ss + RMS_EPS) * qk_scale
        o_ref[0, pl.ds(r0, CONV_ROWS), :] = jnp.where(is_v, y, yn).astype(o_ref.dtype)
        return carry

    lax.fori_loop(0, t_len // CONV_ROWS, body, 0)


def _gdn_conv(p, conv_w, col0):
    bsz, t_len, _ = p.shape
    nblk = GDN_QKV_W // LANES
    cb = col0 // LANES
    return pl.pallas_call(
        _gdn_conv_kernel,
        out_shape=jax.ShapeDtypeStruct((bsz, t_len, GDN_QKV_W), BF16),
        grid=(bsz, nblk),
        in_specs=[pl.BlockSpec((1, t_len, LANES), lambda bi, j: (bi, 0, cb + j)),
                  pl.BlockSpec((GDN_CONV, LANES), lambda bi, j: (0, j))],
        out_specs=pl.BlockSpec((1, t_len, LANES), lambda bi, j: (bi, 0, j)),
        scratch_shapes=[pltpu.VMEM((t_len + 2 * CONV_HALO, LANES), F32)],
        compiler_params=_cparams("arbitrary", "arbitrary"),
        name="gdn_conv",
    )(p, conv_w)


M_EYE, M_INCL, M_NEG_STRICT, M_DIAG_BLOCK, M_MERGE = 0, 1, 3, 5, 6
GDN_MERGE_LEVELS = int(math.log2(GDN_CHUNK // GDN_SERIES_BLOCK))
GDN_WIDE_STEPS = int(math.log2(GDN_SERIES_BLOCK)) - 2
GDN_TILE = 1024
GDN_INTRA_GROUP = 4
GATED_NORM_ROWS = 512


def _gdn_masks():
    c = GDN_CHUNK
    i = np.arange(c)[:, None]
    j = np.arange(c)[None, :]
    rows = [i == j, j <= i, j >= i, -(j < i).astype(np.float32), -(j > i).astype(np.float32),
            (i // GDN_SERIES_BLOCK) == (j // GDN_SERIES_BLOCK)]
    s = GDN_SERIES_BLOCK
    while s < c:
        rows.append(((i // (2 * s)) == (j // (2 * s))) & ((i // s) != (j // s)))
        s *= 2
    return jnp.asarray(np.stack([np.asarray(r, np.float32) for r in rows]))


def _gdn_tri():
    c = GDN_CHUNK
    i = np.arange(c)[:, None]
    j = np.arange(c)[None, :]
    return jnp.asarray(np.concatenate([j <= i, j >= i, np.ones((c, c), bool)], axis=0).astype(np.float32), BF16)


def _gdn_intra(chunks, m_ref, u_ref, wq_ref, akt_ref, et_ref):
    c = GDN_CHUNK
    gqs = [_dot_nt(jnp.concatenate([k, q], axis=0), k) for q, k, _, _, _, _ in chunks]
    eye = m_ref[M_EYE]
    chains = []
    for (q, k, v, table, cum_rows, slot), gq in zip(chunks, gqs):
        for d in range(2):
            def col(base, table=table, d=d):
                lane = base + d * DIR_STRIDE
                return table[:, lane:lane + 1]

            beta_c, cum_c = col(GATE_BETA), col(GATE_CUM)
            incl = m_ref[M_INCL + d]
            ex = jnp.exp((cum_c - cum_rows[d]) * incl)
            neg_m = (gq[:c] * beta_c) * (ex * m_ref[M_NEG_STRICT + d])
            chains.append(dict(q=q, k=k, v=v, table=table, slot=slot, d=d, col=col, beta_c=beta_c,
                               ex_incl=ex * incl, qk=gq[c:], neg_m=neg_m))
    ps = [ch["neg_m"] * m_ref[M_DIAG_BLOCK] for ch in chains]
    sums = [eye + p for p in ps]
    pbs = [p.astype(BF16) for p in ps]
    ps = [_dot(pb, pb) for pb in pbs]
    for _ in range(GDN_WIDE_STEPS):
        pbs = [p.astype(BF16) for p in ps]
        boths = [_dot(pb, jnp.concatenate([pb, s.astype(BF16)], axis=1)) for pb, s in zip(pbs, sums)]
        ps = [both[:, :c] for both in boths]
        sums = [s + both[:, c:] for s, both in zip(sums, boths)]
    invs = [s + _dot(p.astype(BF16), s.astype(BF16)) for p, s in zip(ps, sums)]
    for lvl in range(GDN_MERGE_LEVELS):
        ibs = [inv.astype(BF16) for inv in invs]
        offs = [(ch["neg_m"] * m_ref[M_MERGE + lvl]).astype(BF16) for ch in chains]
        halves = [_dot(ib, off).astype(BF16) for ib, off in zip(ibs, offs)]
        invs = [inv + _dot(half, ib) for inv, half, ib in zip(invs, halves, ibs)]
    rhss = []
    for ch in chains:
        kf = ch["k"].astype(F32)
        vf = ch["v"].astype(F32)
        e_c = ch["col"](GATE_E)
        rhss.append(jnp.concatenate([(vf * ch["beta_c"]).astype(BF16), (kf * (ch["beta_c"] * e_c)).astype(BF16)], axis=1))
    uws = [_dot(inv.astype(BF16), rhs) for inv, rhs in zip(invs, rhss)]
    for ch, uw in zip(chains, uws):
        d, slot = ch["d"], ch["slot"]
        a = (ch["qk"] * ch["ex_incl"]).astype(BF16)
        qd = (ch["q"].astype(F32) * ch["col"](GATE_E)).astype(BF16)
        kdt = (ch["k"].astype(F32) * ch["col"](GATE_EK)).T.astype(BF16)
        u_ref[d, slot] = uw[:, :c].astype(BF16)
        wq_ref[d, slot] = jnp.concatenate([uw[:, c:].astype(BF16), qd], axis=0)
        akt_ref[d, slot] = jnp.concatenate([a, kdt], axis=0)
        lane = GATE_ETOT + d * DIR_STRIDE
        et_ref[d, slot] = jnp.broadcast_to(ch["table"][0:1, lane:lane + 1], (SUBLANES, LANES))


def _gdn_scan(u_ref, wq_ref, akt_ref, et_ref, s_scrs, o_scr, n_chunk):
    c = GDN_CHUNK

    def body(n, carry):
        idxs = (n, n_chunk - 1 - n)
        ss = [s_scrs[d][...] for d in range(2)]
        ws_qs = [_dot(wq_ref[d, idxs[d]], ss[d].astype(BF16)) for d in range(2)]
        v_news = [(u_ref[d, idxs[d]].astype(F32) - ws_qs[d][:c]).astype(BF16) for d in range(2)]
        av_kvs = [_dot(akt_ref[d, idxs[d]], v_news[d]) for d in range(2)]
        for d in range(2):
            r0 = pl.multiple_of(idxs[d] * c, c)
            o_scr[pl.ds(r0, c), :] += ws_qs[d][c:] + av_kvs[d][:c]
            s_scrs[d][...] = ss[d] * et_ref[d, idxs[d]][0:1, :] + av_kvs[d][c:]
        return carry

    lax.fori_loop(0, n_chunk, body, 0)


def _gated_norm(o_scr, z_ref, nw, y_ref):
    n_rows = o_scr.shape[0]
    rows = min(GATED_NORM_ROWS, n_rows)

    def body(c, carry):
        r0 = pl.multiple_of(c * rows, rows)
        o = o_scr[pl.ds(r0, rows), :]
        z = z_ref[0, pl.ds(r0, rows), :]
        on = o * lax.rsqrt(jnp.mean(o * o, axis=-1, keepdims=True) + RMS_EPS) * nw
        y_ref[0, pl.ds(r0, rows), :] = (on * (z * jax.nn.sigmoid(z))).astype(y_ref.dtype)
        return carry

    lax.fori_loop(0, n_rows // rows, body, 0)


def _gdn_kernel(ql_ref, kl_ref, vl_ref, tl_ref, rfl_ref, rbl_ref,
                qc_ref, kc_ref, vc_ref, tc_ref, rfc_ref, rbc_ref,
                m_ref, zl_ref, zc_ref, nw_ref, yl_ref, yc_ref,
                ul, wql, aktl, etl, uc, wqc, aktc, etc, ol_scr, oc_scr, sf_scr, sb_scr):
    c = GDN_CHUNK
    i = pl.program_id(2)
    n_tiles = pl.num_programs(2) - 1
    chunks_per_tile = ql_ref.shape[1] // c
    n_ctx = qc_ref.shape[1] // c
    n_lat = ul.shape[1]
    s_scrs = (sf_scr, sb_scr)

    @pl.when(i == 0)
    def _():
        sf_scr[...] = jnp.zeros_like(sf_scr)
        sb_scr[...] = jnp.zeros_like(sb_scr)
        oc_scr[...] = jnp.zeros_like(oc_scr)
        ol_scr[...] = jnp.zeros_like(ol_scr)
        for n0 in range(0, n_ctx, GDN_INTRA_GROUP):
            chunks = []
            for n in range(n0, min(n0 + GDN_INTRA_GROUP, n_ctx)):
                rs = slice(n * c, (n + 1) * c)
                chunks.append((qc_ref[0, rs, :], kc_ref[0, rs, :], vc_ref[0, rs, :], tc_ref[0, 0, rs, :],
                               (rfc_ref[0, 0, n:n + 1, :], rbc_ref[0, 0, n:n + 1, :]), n))
            _gdn_intra(chunks, m_ref, uc, wqc, aktc, etc)
        _gdn_scan(uc, wqc, aktc, etc, s_scrs, oc_scr, n_ctx)

    @pl.when(i < n_tiles)
    def _():
        group = math.gcd(GDN_INTRA_GROUP, chunks_per_tile)

        def body(gi, carry):
            chunks = []
            for k in range(group):
                cn = gi * group + k
                r0 = pl.multiple_of(cn * c, c)
                chunks.append((ql_ref[0, pl.ds(r0, c), :], kl_ref[0, pl.ds(r0, c), :], vl_ref[0, pl.ds(r0, c), :],
                               tl_ref[0, 0, pl.ds(r0, c), :],
                               (rfl_ref[0, 0, pl.ds(cn, 1), :], rbl_ref[0, 0, pl.ds(cn, 1), :]),
                               i * chunks_per_tile + cn))
            _gdn_intra(chunks, m_ref, ul, wql, aktl, etl)
            return carry

        lax.fori_loop(0, chunks_per_tile // group, body, 0)

    @pl.when(i == n_tiles)
    def _():
        _gdn_scan(ul, wql, aktl, etl, s_scrs, ol_scr, n_lat)
        nw = nw_ref[...]
        _gated_norm(ol_scr, zl_ref, nw, yl_ref)
        _gated_norm(oc_scr, zc_ref, nw, yc_ref)


def _gdn(qkv_l, qkv_c, tab_l, rows_l, tab_c, rows_c, masks, p_l, p_c, z_col0, norm_w):
    bsz, t_len, _ = qkv_l.shape
    l_len = qkv_c.shape[1]
    c = GDN_CHUNK
    nh = GDN_HEADS
    tb = _pick_tile(t_len, GDN_TILE)
    n_tiles = t_len // tb
    n_lat, n_ctx = t_len // c, l_len // c
    rows_l = rows_l.reshape(bsz, 2 * nh, n_lat, c)
    rows_c = rows_c.reshape(bsz, 2 * nh, n_ctx, c)
    zb = z_col0 // LANES
    tile = lambda i: jnp.minimum(i, n_tiles - 1)

    def lat_spec(col_off):
        return pl.BlockSpec((1, tb, LANES), lambda bi, h, i: (bi, tile(i), col_off + h))

    def ctx_spec(col_off):
        return pl.BlockSpec((1, l_len, LANES), lambda bi, h, i: (bi, 0, col_off + h))

    def full_spec(n_rows, col_off):
        return pl.BlockSpec((1, n_rows, LANES), lambda bi, h, i: (bi, 0, col_off + h))

    in_specs = [lat_spec(0), lat_spec(nh), lat_spec(2 * nh),
                pl.BlockSpec((1, 1, tb, LANES), lambda bi, h, i: (bi, h, tile(i), 0)),
                pl.BlockSpec((1, 1, tb // c, c), lambda bi, h, i: (bi, h, tile(i), 0)),
                pl.BlockSpec((1, 1, tb // c, c), lambda bi, h, i: (bi, nh + h, tile(i), 0)),
                ctx_spec(0), ctx_spec(nh), ctx_spec(2 * nh),
                pl.BlockSpec((1, 1, l_len, LANES), lambda bi, h, i: (bi, h, 0, 0)),
                pl.BlockSpec((1, 1, n_ctx, c), lambda bi, h, i: (bi, h, 0, 0)),
                pl.BlockSpec((1, 1, n_ctx, c), lambda bi, h, i: (bi, nh + h, 0, 0)),
                pl.BlockSpec(masks.shape, lambda bi, h, i: (0, 0, 0)),
                full_spec(t_len, zb), full_spec(l_len, zb),
                pl.BlockSpec((1, LANES), lambda bi, h, i: (0, 0))]

    def scratch(n_chunk):
        return [pltpu.VMEM((2, n_chunk, c, GDN_DV), BF16), pltpu.VMEM((2, n_chunk, 2 * c, GDN_DK), BF16),
                pltpu.VMEM((2, n_chunk, 2 * c, c), BF16), pltpu.VMEM((2, n_chunk, SUBLANES, LANES), F32)]

    return pl.pallas_call(
        _gdn_kernel,
        out_shape=(jax.ShapeDtypeStruct((bsz, t_len, GDN_V_W), BF16),
                   jax.ShapeDtypeStruct((bsz, l_len, GDN_V_W), BF16)),
        grid=(bsz, nh, n_tiles + 1),
        in_specs=in_specs,
        out_specs=(full_spec(t_len, 0), full_spec(l_len, 0)),
        scratch_shapes=scratch(n_lat) + scratch(n_ctx) + [
            pltpu.VMEM((t_len, GDN_DV), F32), pltpu.VMEM((l_len, GDN_DV), F32),
            pltpu.VMEM((GDN_DK, GDN_DV), F32), pltpu.VMEM((GDN_DK, GDN_DV), F32)],
        compiler_params=_cparams("arbitrary", "arbitrary", "arbitrary"),
        name="gdn_scan",
    )(qkv_l, qkv_l, qkv_l, tab_l, rows_l, rows_l, qkv_c, qkv_c, qkv_c, tab_c, rows_c, rows_c,
      masks, p_l, p_c, norm_w)


def _dup_groups(x):
    lane = lax.broadcasted_iota(jnp.int32, (x.shape[0], LANES), 1)
    lo = lane < LANES // 2
    outs = []
    for c in range(x.shape[1] // LANES):
        xc = x[:, c * LANES:(c + 1) * LANES]
        xs = pltpu.roll(xc, LANES // 2, axis=1)
        outs.append(jnp.where(lo, xc, xs))
        outs.append(jnp.where(lo, xs, xc))
    return outs


def _rope_kernel(q_ref, k_ref, v_ref, cos_ref, sin_ref, qo_ref, kt_ref, vo_ref, *, rope):
    q = q_ref[0]
    k = k_ref[0]
    if rope:
        cos = cos_ref[...]
        sin = sin_ref[...]
        quarter = SWA_HEAD_DIM // 2
        lane = lax.broadcasted_iota(jnp.int32, cos.shape, 1)
        first = (lane % SWA_HEAD_DIM) < quarter

        def rot(x):
            outs = []
            for c in range(x.shape[1] // LANES):
                xc = x[:, c * LANES:(c + 1) * LANES]
                partner = jnp.where(first, pltpu.roll(xc, LANES - quarter, axis=1), pltpu.roll(xc, quarter, axis=1))
                outs.append(xc * cos + partner * sin)
            return jnp.concatenate(outs, axis=1)

        q = rot(q)
        k = rot(k)
    qo_ref[0] = (q * (SWA_HEAD_DIM ** -0.5)).astype(qo_ref.dtype)
    for g, kg in enumerate(_dup_groups(k)):
        kt_ref[0, g * LANES:(g + 1) * LANES, :] = kg.T.astype(kt_ref.dtype)
    vo_ref[0] = jnp.concatenate(_dup_groups(v_ref[0]), axis=1).astype(vo_ref.dtype)


def _swa_prep(p, q_col0, k_col0, v_col0, cos_t, sin_t, rope):
    bsz, t_len, _ = p.shape
    tr = _pick_tile(t_len, 512)
    return pl.pallas_call(
        functools.partial(_rope_kernel, rope=rope),
        out_shape=(jax.ShapeDtypeStruct((bsz, t_len, SWA_Q_W), BF16),
                   jax.ShapeDtypeStruct((bsz, 2 * SWA_KV_W, t_len), BF16),
                   jax.ShapeDtypeStruct((bsz, t_len, 2 * SWA_KV_W), BF16)),
        grid=(bsz, t_len // tr),
        in_specs=[pl.BlockSpec((1, tr, SWA_Q_W), lambda bi, i: (bi, i, q_col0 // SWA_Q_W)),
                  pl.BlockSpec((1, tr, SWA_KV_W), lambda bi, i: (bi, i, k_col0 // SWA_KV_W)),
                  pl.BlockSpec((1, tr, SWA_KV_W), lambda bi, i: (bi, i, v_col0 // SWA_KV_W)),
                  pl.BlockSpec((tr, LANES), lambda bi, i: (i, 0)),
                  pl.BlockSpec((tr, LANES), lambda bi, i: (i, 0))],
        out_specs=(pl.BlockSpec((1, tr, SWA_Q_W), lambda bi, i: (bi, i, 0)),
                   pl.BlockSpec((1, 2 * SWA_KV_W, tr), lambda bi, i: (bi, 0, i)),
                   pl.BlockSpec((1, tr, 2 * SWA_KV_W), lambda bi, i: (bi, i, 0))),
        compiler_params=_cparams("arbitrary", "arbitrary"),
        name="swa_prep",
    )(p, p, p, cos_t, sin_t)


def _rope_tables(t_len):
    rows = t_len // GRID_W
    row = jnp.broadcast_to(jnp.arange(rows)[:, None], (rows, GRID_W)).reshape(t_len).astype(F32)
    col = jnp.broadcast_to(jnp.arange(GRID_W)[None, :], (rows, GRID_W)).reshape(t_len).astype(F32)
    n_freq = SWA_HEAD_DIM // 4
    freq = jnp.power(ROPE_THETA, -jnp.arange(n_freq, dtype=F32) / n_freq)
    ang = jnp.concatenate([row[:, None] * freq, col[:, None] * freq], axis=-1)
    cos = jnp.cos(ang)
    sin = jnp.sin(ang)
    reps = LANES // SWA_HEAD_DIM
    return (jnp.tile(jnp.concatenate([cos, cos], axis=-1), (1, reps)),
            jnp.tile(jnp.concatenate([-sin, sin], axis=-1), (1, reps)))


def _attend(q, key_blocks, sink_ref, o_ref):
    n_rows = q.shape[0]
    lane = lax.broadcasted_iota(jnp.int32, (n_rows, LANES), 1)
    lo = lane < LANES // 2
    hpg = SWA_Q_HEADS // SWA_KV_HEADS
    def group_scores(g):
        gs = slice(g * LANES, (g + 1) * LANES)
        q_parts, sink_parts = [], []
        for r in range(hpg):
            head = g * hpg + r
            qc = q[:, (head // 2) * LANES:(head // 2 + 1) * LANES]
            q_parts.append(jnp.where(lo if head % 2 == 0 else jnp.logical_not(lo), qc, jnp.zeros_like(qc)))
            sink_parts.append(jnp.broadcast_to(sink_ref[0:1, head:head + 1], (n_rows, 1)))
        q4 = jnp.concatenate(q_parts, axis=0)
        sink = jnp.concatenate(sink_parts, axis=0)
        scores = []
        m_tile = None
        for kt, _, mask in key_blocks:
            s = _dot(q4, kt[gs, :])
            if mask is not None:
                s = jnp.where(mask, s, NEG_BIG)
            scores.append(s)
            for t in range(s.shape[1] // LANES):
                st = s[:, t * LANES:(t + 1) * LANES]
                m_tile = st if m_tile is None else jnp.maximum(m_tile, st)
        return sink, scores, m_tile

    def group_finish(g, sink, scores, m_tile):
        gs = slice(g * LANES, (g + 1) * LANES)
        m = jnp.maximum(sink, jnp.max(m_tile, axis=-1, keepdims=True))
        acc = jnp.zeros((hpg * n_rows, 2 * LANES), F32)
        for s, (_, v, _) in zip(scores, key_blocks):
            pr = jnp.exp(s - m).astype(BF16)
            v_ones = jnp.concatenate([v[:, gs], jnp.ones((v.shape[0], LANES), BF16)], axis=1)
            acc = acc + _dot(pr, v_ones)
        o = acc[:, :LANES] * (1.0 / (acc[:, LANES:] + jnp.exp(sink - m)))
        for pair in range(hpg // 2):
            col = (g * hpg) // 2 + pair
            even = o[(2 * pair) * n_rows:(2 * pair + 1) * n_rows]
            odd = o[(2 * pair + 1) * n_rows:(2 * pair + 2) * n_rows]
            o_ref[0, :, col * LANES:(col + 1) * LANES] = jnp.where(lo, even, odd).astype(o_ref.dtype)

    pending = group_scores(0)
    for g in range(SWA_KV_HEADS):
        following = group_scores(g + 1) if g + 1 < SWA_KV_HEADS else None
        group_finish(g, *pending)
        pending = following


def _swa_latent_kernel(q_ref, kp_ref, kc_ref, kn_ref, vp_ref, vc_ref, vn_ref, kx_ref, vx_ref, sink_ref, o_ref):
    i = pl.program_id(1)
    nblk = pl.num_programs(1)
    blk = q_ref.shape[1]
    rows4 = (SWA_Q_HEADS // SWA_KV_HEADS) * blk
    qi = lax.broadcasted_iota(jnp.int32, (rows4, blk), 0) & (blk - 1)
    kj = lax.broadcasted_iota(jnp.int32, (rows4, blk), 1)
    mask_prev = kj >= qi + jnp.where(i > 0, 0, blk)
    mask_next = kj <= qi - jnp.where(i < nblk - 1, 0, blk)
    blocks = [(kp_ref[0], vp_ref[0], mask_prev), (kc_ref[0], vc_ref[0], None),
              (kn_ref[0], vn_ref[0], mask_next), (kx_ref[0], vx_ref[0], None)]
    _attend(q_ref[0], blocks, sink_ref, o_ref)


def _swa_latent(q, kt, v2, kt_c, v2_c, sinks):
    bsz, t_len, _ = q.shape
    l_len = v2_c.shape[1]
    blk = SWA_BLOCK
    assert blk & (blk - 1) == 0
    nblk = t_len // blk
    kvw = v2.shape[2]
    prev = lambda i: jnp.maximum(i - 1, 0)
    nxt = lambda i: jnp.minimum(i + 1, nblk - 1)

    def kspec(f):
        return pl.BlockSpec((1, kvw, blk), lambda bi, i: (bi, 0, f(i)))

    def vspec(f):
        return pl.BlockSpec((1, blk, kvw), lambda bi, i: (bi, f(i), 0))

    same = lambda i: i
    return pl.pallas_call(
        _swa_latent_kernel,
        out_shape=jax.ShapeDtypeStruct((bsz, t_len, SWA_Q_W), BF16),
        grid=(bsz, nblk),
        in_specs=[pl.BlockSpec((1, blk, SWA_Q_W), lambda bi, i: (bi, i, 0)),
                  kspec(prev), kspec(same), kspec(nxt), vspec(prev), vspec(same), vspec(nxt),
                  pl.BlockSpec((1, kvw, l_len), lambda bi, i: (bi, 0, 0)),
                  pl.BlockSpec((1, l_len, kvw), lambda bi, i: (bi, 0, 0)),
                  pl.BlockSpec((1, SWA_Q_HEADS), lambda bi, i: (0, 0))],
        out_specs=pl.BlockSpec((1, blk, SWA_Q_W), lambda bi, i: (bi, i, 0)),
        compiler_params=_cparams("arbitrary", "arbitrary"),
        name="swa_latent",
    )(q, kt, kt, kt, v2, v2, v2, kt_c, v2_c, sinks)


def _swa_context_kernel(q_ref, k_ref, v_ref, sink_ref, o_ref):
    _attend(q_ref[0], [(k_ref[0], v_ref[0], None)], sink_ref, o_ref)


def _swa_context(q, kt, v2, sinks):
    bsz, l_len, _ = q.shape
    kvw = v2.shape[2]
    return pl.pallas_call(
        _swa_context_kernel,
        out_shape=jax.ShapeDtypeStruct((bsz, l_len, SWA_Q_W), BF16),
        grid=(bsz,),
        in_specs=[pl.BlockSpec((1, l_len, SWA_Q_W), lambda bi: (bi, 0, 0)),
                  pl.BlockSpec((1, kvw, l_len), lambda bi: (bi, 0, 0)),
                  pl.BlockSpec((1, l_len, kvw), lambda bi: (bi, 0, 0)),
                  pl.BlockSpec((1, SWA_Q_HEADS), lambda bi: (0, 0))],
        out_specs=pl.BlockSpec((1, l_len, SWA_Q_W), lambda bi: (bi, 0, 0)),
        compiler_params=_cparams("arbitrary"),
        name="swa_context",
    )(q, kt, v2, sinks)


def _sgu_kernel(u_ref, v_ref, g_ref, b_ref, ws_ref, bs_ref, o_ref):
    n_rows = u_ref.shape[1]
    v = _layer_norm_rows(_gelu_tanh(v_ref[0]), g_ref[...], b_ref[...]).astype(BF16)
    cw = v.shape[1] // SGU_GROUPS
    for c in range(n_rows // SGU_CHUNK):
        rs = slice(c * SGU_CHUNK, (c + 1) * SGU_CHUNK)
        for g in range(SGU_GROUPS):
            cs = slice(g * cw, (g + 1) * cw)
            s = _dot(ws_ref[g], v[rs, cs]) + bs_ref[:, g:g + 1]
            o_ref[0, rs, cs] = (_gelu_tanh(u_ref[0, rs, cs]) * s).astype(o_ref.dtype)


def _sgu(p, u_col0, v_col0, ln_g, ln_b, ws, bs_t):
    bsz, t_len, _ = p.shape
    w = ln_g.shape[1]
    tr = _pick_tile(t_len, 2 * SGU_CHUNK)
    full = lambda bi, i: (0, 0)
    return pl.pallas_call(
        _sgu_kernel,
        out_shape=jax.ShapeDtypeStruct((bsz, t_len, w), BF16),
        grid=(bsz, t_len // tr),
        in_specs=[pl.BlockSpec((1, tr, w), lambda bi, i: (bi, i, u_col0 // w)),
                  pl.BlockSpec((1, tr, w), lambda bi, i: (bi, i, v_col0 // w)),
                  pl.BlockSpec((1, w), full), pl.BlockSpec((1, w), full),
                  pl.BlockSpec(ws.shape, lambda bi, i: (0, 0, 0)),
                  pl.BlockSpec(bs_t.shape, full)],
        out_specs=pl.BlockSpec((1, tr, w), lambda bi, i: (bi, i, 0)),
        compiler_params=_cparams("arbitrary", "arbitrary"),
        name="sgu",
    )(p, p, ln_g, ln_b, ws, bs_t)


def _merge_kernel(ya_ref, yb_ref, yc_ref, ga_ref, gb_ref, gc_ref, wa_ref, wb_ref, wc_ref, o_ref):
    y = (jax.nn.sigmoid(ga_ref[0]) * _dot(ya_ref[0], wa_ref[...])
         + jax.nn.sigmoid(gb_ref[0]) * _dot(yb_ref[0], wb_ref[...])
         + jax.nn.sigmoid(gc_ref[0]) * _dot(yc_ref[0], wc_ref[...]))
    o_ref[0] = y.astype(o_ref.dtype)


def _merge(ya, yb, yc, p, gate_col0, wa, wb, wc):
    bsz, t_len, _ = ya.shape
    d = wa.shape[1]
    tm = _pick_tile(t_len, 1024)
    tn = _pick_tile(d, 512)
    nb = d // tn
    g0 = gate_col0 // tn

    def yspec(width):
        return pl.BlockSpec((1, tm, width), lambda bi, i, j: (bi, i, 0))

    def gspec(k):
        return pl.BlockSpec((1, tm, tn), lambda bi, i, j: (bi, i, g0 + k * nb + j))

    def wspec(width):
        return pl.BlockSpec((width, tn), lambda bi, i, j: (0, j))

    return pl.pallas_call(
        _merge_kernel,
        out_shape=jax.ShapeDtypeStruct((bsz, t_len, d), BF16),
        grid=(bsz, t_len // tm, nb),
        in_specs=[yspec(ya.shape[2]), yspec(yb.shape[2]), yspec(yc.shape[2]),
                  gspec(0), gspec(1), gspec(2),
                  wspec(wa.shape[0]), wspec(wb.shape[0]), wspec(wc.shape[0])],
        out_specs=pl.BlockSpec((1, tm, tn), lambda bi, i, j: (bi, i, j)),
        compiler_params=_cparams("arbitrary", "arbitrary", "arbitrary"),
        name="merge",
    )(ya, yb, yc, p, p, p, wa, wb, wc)


def _outproj_kernel(y_ref, w_ref, x_ref, mod_ref, g_ref, b_ref, o_ref, *, alpha):
    o = _dot(y_ref[0], w_ref[...])
    r = alpha * x_ref[0] + mod_ref[0, 2:3, :] * o
    o_ref[0] = _layer_norm_rows(r, g_ref[...], b_ref[...])


def _outproj(y, w, x, mod, ln_g, ln_b, alpha):
    bsz, t_len, d = x.shape
    tm = _pick_tile(t_len, 512)
    per_batch = mod.shape[0] > 1
    full = lambda bi, i: (0, 0)
    return pl.pallas_call(
        functools.partial(_outproj_kernel, alpha=alpha),
        out_shape=jax.ShapeDtypeStruct((bsz, t_len, d), F32),
        grid=(bsz, t_len // tm),
        in_specs=[pl.BlockSpec((1, tm, d), lambda bi, i: (bi, i, 0)),
                  pl.BlockSpec((d, d), full),
                  pl.BlockSpec((1, tm, d), lambda bi, i: (bi, i, 0)),
                  pl.BlockSpec((1, N_MOD, d), (lambda bi, i: (bi, 0, 0)) if per_batch else (lambda bi, i: (0, 0, 0))),
                  pl.BlockSpec((1, d), full), pl.BlockSpec((1, d), full)],
        out_specs=pl.BlockSpec((1, tm, d), lambda bi, i: (bi, i, 0)),
        compiler_params=_cparams("arbitrary", "arbitrary"),
        name="out_proj_ln",
    )(y, w, x, mod, ln_g, ln_b)


def _ffn_kernel(x_ref, mod_ref, w1_ref, w2_ref, g_ref, b_ref, o_ref, h_scr, acc_scr, *, alpha):
    j = pl.program_id(2)

    @pl.when(j == 0)
    def _():
        h_scr[...] = (x_ref[0] * (1.0 + mod_ref[0, 4:5, :]) + mod_ref[0, 3:4, :]).astype(BF16)
        acc_scr[...] = jnp.zeros_like(acc_scr)

    a = jnp.maximum(_dot(h_scr[...], w1_ref[...]), 0.0)
    acc_scr[...] += _dot((a * a).astype(BF16), w2_ref[...])

    @pl.when(j == pl.num_programs(2) - 1)
    def _():
        r = alpha * x_ref[0] + mod_ref[0, 5:6, :] * acc_scr[...]
        o_ref[0] = _layer_norm_rows(r, g_ref[...], b_ref[...])


def _ffn(x, mod, w1, w2, ln_g, ln_b, alpha):
    bsz, t_len, d = x.shape
    dff = w1.shape[1]
    tm = _pick_tile(t_len, 512)
    tf = _pick_tile(dff, 1024)
    per_batch = mod.shape[0] > 1
    full = lambda bi, i, j: (0, 0)
    return pl.pallas_call(
        functools.partial(_ffn_kernel, alpha=alpha),
        out_shape=jax.ShapeDtypeStruct((bsz, t_len, d), F32),
        grid=(bsz, t_len // tm, dff // tf),
        in_specs=[pl.BlockSpec((1, tm, d), lambda bi, i, j: (bi, i, 0)),
                  pl.BlockSpec((1, N_MOD, d), (lambda bi, i, j: (bi, 0, 0)) if per_batch else (lambda bi, i, j: (0, 0, 0))),
                  pl.BlockSpec((d, tf), lambda bi, i, j: (0, j)),
                  pl.BlockSpec((tf, d), lambda bi, i, j: (j, 0)),
                  pl.BlockSpec((1, d), full), pl.BlockSpec((1, d), full)],
        out_specs=pl.BlockSpec((1, tm, d), lambda bi, i, j: (bi, i, 0)),
        scratch_shapes=[pltpu.VMEM((tm, d), BF16), pltpu.VMEM((tm, d), F32)],
        compiler_params=_cparams("arbitrary", "arbitrary", "arbitrary"),
        name="ffn_ln",
    )(x, mod, w1, w2, ln_g, ln_b)


def _in_proj_layout(d_model):
    sgu_w = d_model // 2
    src = {}
    off = 0
    for name, width in (("qkv", GDN_QKV_W), ("z", GDN_V_W), ("gdn_gates", 4 * GDN_HEADS),
                        ("swa_q", SWA_Q_W), ("swa_k", SWA_KV_W), ("swa_v", SWA_KV_W),
                        ("sgu_u", sgu_w), ("sgu_v", sgu_w),
                        ("gate_a", d_model), ("gate_b", d_model), ("gate_c", d_model)):
        src[name] = (off, width)
        off += width
    order = ("gate_a", "gate_b", "gate_c", "z", "swa_q", "sgu_u", "sgu_v", "qkv", "swa_k", "swa_v")
    dst = {}
    off = 0
    for name in order:
        dst[name] = off
        assert off % min(src[name][1], 2048) == 0 or name == "qkv"
        off += src[name][1]
    return src, order, dst


def kernel(x, c, ctx, c_ctx, w_ada, b_ada, w_in, b_in, gdn_conv, gdn_a_log, gdn_dt_bias, gdn_norm, swa_sinks, sgu_ln_g, sgu_ln_b, sgu_w, sgu_b, w_branch_a, w_branch_b, w_branch_c, w_out, ln_mix_g, ln_mix_b, w_ff1, w_ff2, ln_ff_g, ln_ff_b):
    bsz, t_len, d = x.shape
    depth = w_ada.shape[0]
    alpha = (2 * depth) ** 0.25
    src, order, dst = _in_proj_layout(d)

    pad_rows = (-(bsz + 1)) % SUBLANES
    cc = jnp.concatenate([c, c_ctx[None, :], jnp.zeros((pad_rows, d), F32)], axis=0)
    mod_all = _ada_mod(cc, w_ada, b_ada).reshape(depth, bsz + 1 + pad_rows, N_MOD, d)
    cos_t, sin_t = _rope_tables(t_len)
    ones_t = jnp.ones((ctx.shape[1], LANES), F32)
    zeros_t = jnp.zeros((ctx.shape[1], LANES), F32)
    masks = _gdn_masks()
    tri = _gdn_tri()
    n_gate = 4 * GDN_HEADS
    lane_pad = LANES - n_gate

    xc = ctx
    for l in range(depth):
        need_ctx = l < depth - 1
        mod_l = mod_all[l, :bsz]
        mod_c = mod_all[l, bsz:bsz + 1]

        def cols(a, name):
            s0, wd = src[name]
            return a[..., s0:s0 + wd]

        w_main = jnp.concatenate([cols(w_in[l], n) for n in order], axis=1).astype(BF16)
        b_main = jnp.concatenate([cols(b_in[l], n) for n in order], axis=0)[None, :]
        w_g = jnp.pad(cols(w_in[l], "gdn_gates"), ((0, 0), (0, lane_pad))).astype(BF16)
        b_g = jnp.pad(cols(b_in[l], "gdn_gates"), (0, lane_pad))[None, :]
        decay_pad = (GATE_RAW_DECAY, LANES - 2 * GATE_RAW_DECAY)
        alog = jnp.pad(gdn_a_log[l].reshape(-1), decay_pad)[None, :]
        dtb = jnp.pad(gdn_dt_bias[l].reshape(-1), decay_pad)[None, :]

        p_l = _inproj(x, mod_l, w_main, b_main)
        p_c = _inproj(xc, mod_c, w_main, b_main)
        tab_l, rows_l = _gdn_gates(x, mod_l, w_g, b_g, alog, dtb, tri)
        tab_c, rows_c = _gdn_gates(xc, mod_c, w_g, b_g, alog, dtb, tri)

        qkv_l = _gdn_conv(p_l, gdn_conv[l], dst["qkv"])
        qkv_c = _gdn_conv(p_c, gdn_conv[l], dst["qkv"])
        ya_l, ya_c = _gdn(qkv_l, qkv_c, tab_l, rows_l, tab_c, rows_c, masks, p_l, p_c, dst["z"],
                          gdn_norm[l][None, :])

        sinks = swa_sinks[l][None, :]
        q_l, kt_l, v2_l = _swa_prep(p_l, dst["swa_q"], dst["swa_k"], dst["swa_v"], cos_t, sin_t, True)
        q_c, kt_c, v2_c = _swa_prep(p_c, dst["swa_q"], dst["swa_k"], dst["swa_v"], ones_t, zeros_t, False)
        yb_l = _swa_latent(q_l, kt_l, v2_l, kt_c, v2_c, sinks)

        ws = sgu_w[l].astype(BF16)
        bs_t = sgu_b[l].T
        sg, sb = sgu_ln_g[l][None, :], sgu_ln_b[l][None, :]
        yc_l = _sgu(p_l, dst["sgu_u"], dst["sgu_v"], sg, sb, ws, bs_t)

        wa, wb, wc = w_branch_a[l].astype(BF16), w_branch_b[l].astype(BF16), w_branch_c[l].astype(BF16)
        wo = w_out[l].astype(BF16)
        w1, w2 = w_ff1[l].astype(BF16), w_ff2[l].astype(BF16)
        lmg, lmb = ln_mix_g[l][None, :], ln_mix_b[l][None, :]
        lfg, lfb = ln_ff_g[l][None, :], ln_ff_b[l][None, :]

        y_l = _merge(ya_l, yb_l, yc_l, p_l, dst["gate_a"], wa, wb, wc)
        x = _outproj(y_l, wo, x, mod_l, lmg, lmb, alpha)
        x = _ffn(x, mod_l, w1, w2, lfg, lfb, alpha)
        if need_ctx:
            yb_c = _swa_context(q_c, kt_c, v2_c, sinks)
            yc_c = _sgu(p_c, dst["sgu_u"], dst["sgu_v"], sg, sb, ws, bs_t)
            y_c = _merge(ya_c, yb_c, yc_c, p_c, dst["gate_a"], wa, wb, wc)
            xc = _outproj(y_c, wo, xc, mod_c, lmg, lmb, alpha)
            xc = _ffn(xc, mod_c, w1, w2, lfg, lfb, alpha)
    return x
```

```python
import functools
import math

import jax
import jax.numpy as jnp
import numpy as np
from jax import lax
from jax.experimental import pallas as pl
from jax.experimental.pallas import tpu as pltpu

F32 = jnp.float32
BF16 = jnp.bfloat16

GRID_W = 64
GDN_HEADS = 8
GDN_DK = 128
GDN_DV = 128
GDN_CONV = 5
SWA_Q_HEADS = 16
SWA_KV_HEADS = 4
SWA_HEAD_DIM = 64
SWA_BLOCK = 128
ROPE_THETA = 10000.0
SGU_GROUPS = 8
SGU_CHUNK = 128
N_MOD = 6
LN_EPS = 1e-5
RMS_EPS = 1e-6

GDN_QK_W = GDN_HEADS * GDN_DK
GDN_V_W = GDN_HEADS * GDN_DV
GDN_QKV_W = 2 * GDN_QK_W + GDN_V_W
SWA_Q_W = SWA_Q_HEADS * SWA_HEAD_DIM
SWA_KV_W = SWA_KV_HEADS * SWA_HEAD_DIM

LANES = 128
SUBLANES = 8
VMEM_LIMIT_BYTES = 56 * 1024 * 1024

GDN_CHUNK = 128
GDN_SERIES_BLOCK = 16
assert GDN_CHUNK == GDN_DK == GDN_DV == LANES
NEG_BIG = -1e30


def _cparams(*sem):
    return pltpu.CompilerParams(dimension_semantics=sem, vmem_limit_bytes=VMEM_LIMIT_BYTES)


def _dot(a, b):
    return jnp.dot(a, b, preferred_element_type=F32)


def _dot_nt(a, b):
    return lax.dot_general(a, b, (((1,), (1,)), ((), ())), preferred_element_type=F32)


def _layer_norm_rows(r, g, b):
    mu = jnp.mean(r, axis=-1, keepdims=True)
    d = r - mu
    var = jnp.mean(d * d, axis=-1, keepdims=True)
    return d * lax.rsqrt(var + LN_EPS) * g + b


def _gelu_tanh(x):
    return x * (0.5 * (1.0 + jnp.tanh(math.sqrt(2.0 / math.pi) * (x + 0.044715 * (x * x * x)))))


def _pick_tile(n, pref):
    t = min(n, pref)
    while n % t:
        t //= 2
    return t


def _ada_kernel(c_ref, w_ref, b_ref, o_ref):
    c = c_ref[...]
    a = (c * jax.nn.sigmoid(c)).astype(BF16)
    o_ref[0] = _dot(a, w_ref[0].astype(BF16)) + b_ref[0]


def _ada_mod(cc, w_ada, b_ada):
    nl, d, n = w_ada.shape
    tn = _pick_tile(n, 1024)
    return pl.pallas_call(
        _ada_kernel,
        out_shape=jax.ShapeDtypeStruct((nl, cc.shape[0], n), F32),
        grid=(nl, n // tn),
        in_specs=[pl.BlockSpec(cc.shape, lambda l, j: (0, 0)),
                  pl.BlockSpec((1, d, tn), lambda l, j: (l, 0, j)),
                  pl.BlockSpec((1, 1, tn), lambda l, j: (l, 0, j))],
        out_specs=pl.BlockSpec((1, cc.shape[0], tn), lambda l, j: (l, 0, j)),
        compiler_params=_cparams("arbitrary", "arbitrary"),
        name="ada_mod",
    )(cc, w_ada, b_ada.reshape(nl, 1, n))


def _inproj_kernel(x_ref, mod_ref, w_ref, b_ref, o_ref, h_scr):
    @pl.when(pl.program_id(2) == 0)
    def _():
        shift = mod_ref[0, 0:1, :]
        scale = mod_ref[0, 1:2, :]
        h_scr[...] = (x_ref[0] * (1.0 + scale) + shift).astype(BF16)

    o_ref[0] = (_dot(h_scr[...], w_ref[...]) + b_ref[...]).astype(o_ref.dtype)


def _inproj(x, mod, w, b):
    bsz, t_len, d = x.shape
    n = w.shape[1]
    tm = _pick_tile(t_len, 1024)
    tn = _pick_tile(n, 1536)
    per_batch = mod.shape[0] > 1
    return pl.pallas_call(
        _inproj_kernel,
        out_shape=jax.ShapeDtypeStruct((bsz, t_len, n), BF16),
        grid=(bsz, t_len // tm, n // tn),
        in_specs=[pl.BlockSpec((1, tm, d), lambda bi, i, j: (bi, i, 0)),
                  pl.BlockSpec((1, N_MOD, d), (lambda bi, i, j: (bi, 0, 0)) if per_batch else (lambda bi, i, j: (0, 0, 0))),
                  pl.BlockSpec((d, tn), lambda bi, i, j: (0, j)),
                  pl.BlockSpec((1, tn), lambda bi, i, j: (0, j))],
        out_specs=pl.BlockSpec((1, tm, tn), lambda bi, i, j: (bi, i, j)),
        scratch_shapes=[pltpu.VMEM((tm, d), BF16)],
        compiler_params=_cparams("arbitrary", "arbitrary", "arbitrary"),
        name="in_proj",
    )(x, mod, w, b)


GATE_BETA, GATE_CUM, GATE_E, GATE_EK, GATE_ETOT = 0, 16, 32, 48, 64
DIR_STRIDE = GDN_HEADS
GATE_RAW_DECAY = 2 * GDN_HEADS


def _split3_bf16(x):
    hi = x.astype(BF16)
    r1 = x - hi.astype(F32)
    mid = r1.astype(BF16)
    lo = (r1 - mid.astype(F32)).astype(BF16)
    return hi, mid, lo


def _gdn_gate_kernel(x_ref, mod_ref, w_ref, b_ref, alog_ref, dtb_ref, tri_ref, cols_ref, rows_ref):
    c_len = GDN_CHUNK
    tm = x_ref.shape[1]
    shift = mod_ref[0, 0:1, :]
    scale = mod_ref[0, 1:2, :]
    h = (x_ref[0] * (1.0 + scale) + shift).astype(BF16)
    st = _dot(h, w_ref[...]) + b_ref[...]
    lane = lax.broadcasted_iota(jnp.int32, (c_len, LANES), 1)
    is_decay = (lane >= GATE_RAW_DECAY) & (lane < 2 * GATE_RAW_DECAY)
    tri = tri_ref[...]
    for c in range(tm // c_len):
        rs = slice(c * c_len, (c + 1) * c_len)
        stc = st[rs]
        beta = jax.nn.sigmoid(stc)
        xs = stc + dtb_ref[...]
        softplus = jnp.maximum(xs, 0.0) + jnp.log1p(jnp.exp(-jnp.abs(xs)))
        g = jnp.where(is_decay, -jnp.exp(alog_ref[...]) * softplus, 0.0)
        t3 = sum(_dot(tri, piece) for piece in _split3_bf16(g))
        tot = t3[2 * c_len:3 * c_len]
        cum = jnp.where(lane >= GATE_RAW_DECAY + DIR_STRIDE, t3[c_len:2 * c_len], t3[0:c_len])
        e = jnp.exp(cum)
        ek = jnp.exp(tot - cum)
        et = jnp.exp(tot)
        assert GATE_CUM == GATE_RAW_DECAY
        table = jnp.where(lane < GATE_CUM, beta,
                          jnp.where(lane < GATE_E, cum,
                                    jnp.where(lane < GATE_EK, pltpu.roll(e, GATE_E - GATE_RAW_DECAY, axis=1),
                                              jnp.where(lane < GATE_ETOT, pltpu.roll(ek, GATE_EK - GATE_RAW_DECAY, axis=1),
                                                        pltpu.roll(et, GATE_ETOT - GATE_RAW_DECAY, axis=1)))))
        for hh in range(GDN_HEADS):
            cols_ref[0, hh, rs, :] = pltpu.roll(table, LANES - hh, axis=1) if hh else table
        rows_ref[0, :, rs] = cum.T[GATE_RAW_DECAY:2 * GATE_RAW_DECAY]


def _gdn_gates(x, mod, w, b, alog, dtb, tri):
    bsz, t_len, d = x.shape
    tm = _pick_tile(t_len, 512)
    per_batch = mod.shape[0] > 1
    small = lambda bi, i: (0, 0)
    return pl.pallas_call(
        _gdn_gate_kernel,
        out_shape=(jax.ShapeDtypeStruct((bsz, GDN_HEADS, t_len, LANES), F32),
                   jax.ShapeDtypeStruct((bsz, 2 * GDN_HEADS, t_len), F32)),
        grid=(bsz, t_len // tm),
        in_specs=[pl.BlockSpec((1, tm, d), lambda bi, i: (bi, i, 0)),
                  pl.BlockSpec((1, N_MOD, d), (lambda bi, i: (bi, 0, 0)) if per_batch else (lambda bi, i: (0, 0, 0))),
                  pl.BlockSpec((d, LANES), small),
                  pl.BlockSpec((1, LANES), small),
                  pl.BlockSpec((1, LANES), small),
                  pl.BlockSpec((1, LANES), small),
                  pl.BlockSpec(tri.shape, small)],
        out_specs=(pl.BlockSpec((1, GDN_HEADS, tm, LANES), lambda bi, i: (bi, 0, i, 0)),
                   pl.BlockSpec((1, 2 * GDN_HEADS, tm), lambda bi, i: (bi, 0, i))),
        compiler_params=_cparams("arbitrary", "arbitrary"),
        name="gdn_gates",
    )(x, mod, w, b, alog, dtb, tri)


CONV_ROWS = 256
CONV_HALO = SUBLANES


def _gdn_conv_kernel(x_ref, w_ref, o_ref, pad_scr):
    t_len = x_ref.shape[1]
    j = pl.program_id(1)
    zeros = jnp.zeros((CONV_HALO, LANES), F32)
    pad_scr[0:CONV_HALO, :] = zeros
    pad_scr[t_len + CONV_HALO:t_len + 2 * CONV_HALO, :] = zeros
    pad_scr[CONV_HALO:t_len + CONV_HALO, :] = x_ref[0].astype(F32)
    heads_qk = 2 * GDN_HEADS
    qk_scale = jnp.where(j < GDN_HEADS, GDN_DK ** -0.5, 1.0).astype(F32)
    w = w_ref[...]
    half = GDN_CONV // 2

    def conv_silu(c):
        r0 = pl.multiple_of(c * CONV_ROWS, CONV_ROWS)
        acc = jnp.zeros((CONV_ROWS, LANES), F32)
        for k in range(GDN_CONV):
            off = CONV_HALO - half + k
            acc = acc + pad_scr[pl.ds(r0 + off, CONV_ROWS), :] * w[k:k + 1, :]
        return r0, acc * jax.nn.sigmoid(acc)

    def body_qk(c, carry):
        r0, y = conv_silu(c)
        ss = jnp.sum(y * y, axis=-1, keepdims=True)
        o_ref[0, pl.ds(r0, CONV_ROWS), :] = (y * (lax.rsqrt(ss + RMS_EPS) * qk_scale)).astype(o_ref.dtype)
        return carry

    def body_v(c, carry):
        r0, y = conv_silu(c)
        o_ref[0, pl.ds(r0, CONV_ROWS), :] = y.astype(o_ref.dtype)
        return carry

    @pl.when(j < heads_qk)
    def _():
        lax.fori_loop(0, t_len // CONV_ROWS, body_qk, 0, unroll=2 if t_len // CONV_ROWS % 2 == 0 else 1)

    @pl.when(j >= heads_qk)
    def _():
        lax.fori_loop(0, t_len // CONV_ROWS, body_v, 0)


def _gdn_conv(p, conv_w, col0):
    bsz, t_len, _ = p.shape
    nblk = GDN_QKV_W // LANES
    cb = col0 // LANES
    return pl.pallas_call(
        _gdn_conv_kernel,
        out_shape=jax.ShapeDtypeStruct((bsz, t_len, GDN_QKV_W), BF16),
        grid=(bsz, nblk),
        in_specs=[pl.BlockSpec((1, t_len, LANES), lambda bi, j: (bi, 0, cb + j)),
                  pl.BlockSpec((GDN_CONV, LANES), lambda bi, j: (0, j))],
        out_specs=pl.BlockSpec((1, t_len, LANES), lambda bi, j: (bi, 0, j)),
        scratch_shapes=[pltpu.VMEM((t_len + 2 * CONV_HALO, LANES), F32)],
        compiler_params=_cparams("arbitrary", "arbitrary"),
        name="gdn_conv",
    )(p, conv_w)


M_EYE, M_INCL, M_NEG_STRICT, M_DIAG_BLOCK, M_MERGE = 0, 1, 3, 5, 6
GDN_MERGE_LEVELS = int(math.log2(GDN_CHUNK // GDN_SERIES_BLOCK))
GDN_WIDE_STEPS = int(math.log2(GDN_SERIES_BLOCK)) - 2
GDN_TILE = 1024
GDN_INTRA_GROUP = 8
GATED_NORM_ROWS = 512


def _gdn_masks():
    c = GDN_CHUNK
    i = np.arange(c)[:, None]
    j = np.arange(c)[None, :]
    rows = [i == j, j <= i, j >= i, -(j < i).astype(np.float32), -(j > i).astype(np.float32),
            (i // GDN_SERIES_BLOCK) == (j // GDN_SERIES_BLOCK)]
    s = GDN_SERIES_BLOCK
    while s < c:
        rows.append(((i // (2 * s)) == (j // (2 * s))) & ((i // s) != (j // s)))
        s *= 2
    return jnp.asarray(np.stack([np.asarray(r, np.float32) for r in rows]))


def _gdn_tri():
    c = GDN_CHUNK
    i = np.arange(c)[:, None]
    j = np.arange(c)[None, :]
    return jnp.asarray(np.concatenate([j <= i, j >= i, np.ones((c, c), bool)], axis=0).astype(np.float32), BF16)


def _gdn_intra(chunks, m_ref, u_ref, wq_ref, akt_ref, et_ref):
    c = GDN_CHUNK
    gqs = [_dot_nt(jnp.concatenate([k, q], axis=0), k) for q, k, _, _, _, _ in chunks]
    eye = m_ref[M_EYE]
    chains = []
    for (q, k, v, table, cum_rows, slot), gq in zip(chunks, gqs):
        for d in range(2):
            def col(base, table=table, d=d):
                lane = base + d * DIR_STRIDE
                return table[:, lane:lane + 1]

            beta_c, cum_c = col(GATE_BETA), col(GATE_CUM)
            incl = m_ref[M_INCL + d]
            ex = jnp.exp((cum_c - cum_rows[d]) * incl)
            neg_m = (gq[:c] * beta_c) * (ex * m_ref[M_NEG_STRICT + d])
            chains.append(dict(q=q, k=k, v=v, table=table, slot=slot, d=d, col=col, beta_c=beta_c,
                               ex_incl=ex * incl, qk=gq[c:], neg_m=neg_m))
    ps = [ch["neg_m"] * m_ref[M_DIAG_BLOCK] for ch in chains]
    sums = [eye + p for p in ps]
    pbs = [p.astype(BF16) for p in ps]
    ps = [_dot(pb, pb) for pb in pbs]
    for _ in range(GDN_WIDE_STEPS):
        pbs = [p.astype(BF16) for p in ps]
        boths = [_dot(pb, jnp.concatenate([pb, s.astype(BF16)], axis=1)) for pb, s in zip(pbs, sums)]
        ps = [both[:, :c] for both in boths]
        sums = [s + both[:, c:] for s, both in zip(sums, boths)]
    invs = [s + _dot(p.astype(BF16), s.astype(BF16)) for p, s in zip(ps, sums)]
    for lvl in range(GDN_MERGE_LEVELS):
        ibs = [inv.astype(BF16) for inv in invs]
        offs = [(ch["neg_m"] * m_ref[M_MERGE + lvl]).astype(BF16) for ch in chains]
        halves = [_dot(ib, off).astype(BF16) for ib, off in zip(ibs, offs)]
        invs = [inv + _dot(half, ib) for inv, half, ib in zip(invs, halves, ibs)]
    rhss = []
    for ch in chains:
        kf = ch["k"].astype(F32)
        vf = ch["v"].astype(F32)
        e_c = ch["col"](GATE_E)
        rhss.append(jnp.concatenate([(vf * ch["beta_c"]).astype(BF16), (kf * (ch["beta_c"] * e_c)).astype(BF16)], axis=1))
    uws = [_dot(inv.astype(BF16), rhs) for inv, rhs in zip(invs, rhss)]
    for ch, uw in zip(chains, uws):
        d, slot = ch["d"], ch["slot"]
        a = (ch["qk"] * ch["ex_incl"]).astype(BF16)
        qd = (ch["q"].astype(F32) * ch["col"](GATE_E)).astype(BF16)
        kdt = (ch["k"].astype(F32) * ch["col"](GATE_EK)).T.astype(BF16)
        u_ref[d, slot] = uw[:, :c].astype(BF16)
        wq_ref[d, slot] = jnp.concatenate([uw[:, c:].astype(BF16), qd], axis=0)
        akt_ref[d, slot] = jnp.concatenate([a, kdt], axis=0)
        lane = GATE_ETOT + d * DIR_STRIDE
        et_ref[d, slot] = jnp.broadcast_to(ch["table"][0:1, lane:lane + 1], (SUBLANES, LANES))


def _gdn_scan(u_ref, wq_ref, akt_ref, et_ref, s_scrs, o_scr, n_chunk):
    c = GDN_CHUNK

    def body(n, carry):
        idxs = (n, n_chunk - 1 - n)
        ss = [s_scrs[d][...] for d in range(2)]
        ws_qs = [_dot(wq_ref[d, idxs[d]], ss[d].astype(BF16)) for d in range(2)]
        v_news = [(u_ref[d, idxs[d]].astype(F32) - ws_qs[d][:c]).astype(BF16) for d in range(2)]
        av_kvs = [_dot(akt_ref[d, idxs[d]], v_news[d]) for d in range(2)]
        for d in range(2):
            r0 = pl.multiple_of(idxs[d] * c, c)
            o_scr[pl.ds(r0, c), :] += ws_qs[d][c:] + av_kvs[d][:c]
            s_scrs[d][...] = ss[d] * et_ref[d, idxs[d]][0:1, :] + av_kvs[d][c:]
        return carry

    lax.fori_loop(0, n_chunk, body, 0)


def _gated_norm(o_scr, z_ref, nw, y_ref):
    n_rows = o_scr.shape[0]
    rows = min(GATED_NORM_ROWS, n_rows)

    def body(c, carry):
        r0 = pl.multiple_of(c * rows, rows)
        o = o_scr[pl.ds(r0, rows), :]
        z = z_ref[0, pl.ds(r0, rows), :].astype(F32)
        on = o * lax.rsqrt(jnp.mean(o * o, axis=-1, keepdims=True) + RMS_EPS) * nw
        y_ref[0, pl.ds(r0, rows), :] = (on * (z * jax.nn.sigmoid(z))).astype(y_ref.dtype)
        return carry

    lax.fori_loop(0, n_rows // rows, body, 0)


def _gdn_kernel(ql_ref, kl_ref, vl_ref, tl_ref, rfl_ref, rbl_ref,
                qc_ref, kc_ref, vc_ref, tc_ref, rfc_ref, rbc_ref,
                m_ref, zl_ref, zc_ref, nw_ref, yl_ref, yc_ref,
                ul, wql, aktl, etl, uc, wqc, aktc, etc, ol_scr, oc_scr, sf_scr, sb_scr):
    c = GDN_CHUNK
    i = pl.program_id(2)
    n_tiles = pl.num_programs(2) - 1
    chunks_per_tile = ql_ref.shape[1] // c
    n_ctx = qc_ref.shape[1] // c
    n_lat = ul.shape[1]
    s_scrs = (sf_scr, sb_scr)

    @pl.when(i == 0)
    def _():
        sf_scr[...] = jnp.zeros_like(sf_scr)
        sb_scr[...] = jnp.zeros_like(sb_scr)
        oc_scr[...] = jnp.zeros_like(oc_scr)
        ol_scr[...] = jnp.zeros_like(ol_scr)
        for n0 in range(0, n_ctx, GDN_INTRA_GROUP):
            chunks = []
            for n in range(n0, min(n0 + GDN_INTRA_GROUP, n_ctx)):
                rs = slice(n * c, (n + 1) * c)
                chunks.append((qc_ref[0, rs, :], kc_ref[0, rs, :], vc_ref[0, rs, :], tc_ref[0, 0, rs, :],
                               (rfc_ref[0, 0, n:n + 1, :], rbc_ref[0, 0, n:n + 1, :]), n))
            _gdn_intra(chunks, m_ref, uc, wqc, aktc, etc)
        _gdn_scan(uc, wqc, aktc, etc, s_scrs, oc_scr, n_ctx)

    @pl.when(i < n_tiles)
    def _():
        group = math.gcd(GDN_INTRA_GROUP, chunks_per_tile)

        def body(gi, carry):
            chunks = []
            for k in range(group):
                cn = gi * group + k
                r0 = pl.multiple_of(cn * c, c)
                chunks.append((ql_ref[0, pl.ds(r0, c), :], kl_ref[0, pl.ds(r0, c), :], vl_ref[0, pl.ds(r0, c), :],
                               tl_ref[0, 0, pl.ds(r0, c), :],
                               (rfl_ref[0, 0, pl.ds(cn, 1), :], rbl_ref[0, 0, pl.ds(cn, 1), :]),
                               i * chunks_per_tile + cn))
            _gdn_intra(chunks, m_ref, ul, wql, aktl, etl)
            return carry

        lax.fori_loop(0, chunks_per_tile // group, body, 0)

    @pl.when(i == n_tiles)
    def _():
        _gdn_scan(ul, wql, aktl, etl, s_scrs, ol_scr, n_lat)
        nw = nw_ref[...]
        _gated_norm(ol_scr, zl_ref, nw, yl_ref)
        _gated_norm(oc_scr, zc_ref, nw, yc_ref)


def _gdn(qkv_l, qkv_c, tab_l, rows_l, tab_c, rows_c, masks, p_l, p_c, z_col0, norm_w):
    bsz, t_len, _ = qkv_l.shape
    l_len = qkv_c.shape[1]
    c = GDN_CHUNK
    nh = GDN_HEADS
    tb = _pick_tile(t_len, GDN_TILE)
    n_tiles = t_len // tb
    n_lat, n_ctx = t_len // c, l_len // c
    rows_l = rows_l.reshape(bsz, 2 * nh, n_lat, c)
    rows_c = rows_c.reshape(bsz, 2 * nh, n_ctx, c)
    zb = z_col0 // LANES
    tile = lambda i: jnp.minimum(i, n_tiles - 1)

    def lat_spec(col_off):
        return pl.BlockSpec((1, tb, LANES), lambda bi, h, i: (bi, tile(i), col_off + h))

    def ctx_spec(col_off):
        return pl.BlockSpec((1, l_len, LANES), lambda bi, h, i: (bi, 0, col_off + h))

    def full_spec(n_rows, col_off):
        return pl.BlockSpec((1, n_rows, LANES), lambda bi, h, i: (bi, 0, col_off + h))

    in_specs = [lat_spec(0), lat_spec(nh), lat_spec(2 * nh),
                pl.BlockSpec((1, 1, tb, LANES), lambda bi, h, i: (bi, h, tile(i), 0)),
                pl.BlockSpec((1, 1, tb // c, c), lambda bi, h, i: (bi, h, tile(i), 0)),
                pl.BlockSpec((1, 1, tb // c, c), lambda bi, h, i: (bi, nh + h, tile(i), 0)),
                ctx_spec(0), ctx_spec(nh), ctx_spec(2 * nh),
                pl.BlockSpec((1, 1, l_len, LANES), lambda bi, h, i: (bi, h, 0, 0)),
                pl.BlockSpec((1, 1, n_ctx, c), lambda bi, h, i: (bi, h, 0, 0)),
                pl.BlockSpec((1, 1, n_ctx, c), lambda bi, h, i: (bi, nh + h, 0, 0)),
                pl.BlockSpec(masks.shape, lambda bi, h, i: (0, 0, 0)),
                full_spec(t_len, zb), full_spec(l_len, zb),
                pl.BlockSpec((1, LANES), lambda bi, h, i: (0, 0))]

    def scratch(n_chunk):
        return [pltpu.VMEM((2, n_chunk, c, GDN_DV), BF16), pltpu.VMEM((2, n_chunk, 2 * c, GDN_DK), BF16),
                pltpu.VMEM((2, n_chunk, 2 * c, c), BF16), pltpu.VMEM((2, n_chunk, SUBLANES, LANES), F32)]

    return pl.pallas_call(
        _gdn_kernel,
        out_shape=(jax.ShapeDtypeStruct((bsz, t_len, GDN_V_W), BF16),
                   jax.ShapeDtypeStruct((bsz, l_len, GDN_V_W), BF16)),
        grid=(bsz, nh, n_tiles + 1),
        in_specs=in_specs,
        out_specs=(full_spec(t_len, 0), full_spec(l_len, 0)),
        scratch_shapes=scratch(n_lat) + scratch(n_ctx) + [
            pltpu.VMEM((t_len, GDN_DV), F32), pltpu.VMEM((l_len, GDN_DV), F32),
            pltpu.VMEM((GDN_DK, GDN_DV), F32), pltpu.VMEM((GDN_DK, GDN_DV), F32)],
        compiler_params=_cparams("arbitrary", "arbitrary", "arbitrary"),
        name="gdn_scan",
    )(qkv_l, qkv_l, qkv_l, tab_l, rows_l, rows_l, qkv_c, qkv_c, qkv_c, tab_c, rows_c, rows_c,
      masks, p_l, p_c, norm_w)


def _dup_groups(x):
    lane = lax.broadcasted_iota(jnp.int32, (x.shape[0], LANES), 1)
    lo = lane < LANES // 2
    outs = []
    for c in range(x.shape[1] // LANES):
        xc = x[:, c * LANES:(c + 1) * LANES]
        xs = pltpu.roll(xc, LANES // 2, axis=1)
        outs.append(jnp.where(lo, xc, xs))
        outs.append(jnp.where(lo, xs, xc))
    return outs


def _rope_kernel(q_ref, k_ref, v_ref, cos_ref, sin_ref, qo_ref, kt_ref, vo_ref, *, rope):
    q = q_ref[0].astype(F32)
    k = k_ref[0].astype(F32)
    if rope:
        cos = cos_ref[...]
        sin = sin_ref[...]
        quarter = SWA_HEAD_DIM // 2
        lane = lax.broadcasted_iota(jnp.int32, cos.shape, 1)
        first = (lane % SWA_HEAD_DIM) < quarter

        def rot(x):
            outs = []
            for c in range(x.shape[1] // LANES):
                xc = x[:, c * LANES:(c + 1) * LANES]
                partner = jnp.where(first, pltpu.roll(xc, LANES - quarter, axis=1), pltpu.roll(xc, quarter, axis=1))
                outs.append(xc * cos + partner * sin)
            return jnp.concatenate(outs, axis=1)

        q = rot(q)
        k = rot(k)
    qo_ref[0] = (q * (SWA_HEAD_DIM ** -0.5)).astype(qo_ref.dtype)
    for g, kg in enumerate(_dup_groups(k)):
        kt_ref[0, g * LANES:(g + 1) * LANES, :] = kg.T.astype(kt_ref.dtype)
    vo_ref[0] = jnp.concatenate(_dup_groups(v_ref[0].astype(F32)), axis=1).astype(vo_ref.dtype)


def _swa_prep(p, q_col0, k_col0, v_col0, cos_t, sin_t, rope):
    bsz, t_len, _ = p.shape
    tr = _pick_tile(t_len, 512)
    return pl.pallas_call(
        functools.partial(_rope_kernel, rope=rope),
        out_shape=(jax.ShapeDtypeStruct((bsz, t_len, SWA_Q_W), BF16),
                   jax.ShapeDtypeStruct((bsz, 2 * SWA_KV_W, t_len), BF16),
                   jax.ShapeDtypeStruct((bsz, t_len, 2 * SWA_KV_W), BF16)),
        grid=(bsz, t_len // tr),
        in_specs=[pl.BlockSpec((1, tr, SWA_Q_W), lambda bi, i: (bi, i, q_col0 // SWA_Q_W)),
                  pl.BlockSpec((1, tr, SWA_KV_W), lambda bi, i: (bi, i, k_col0 // SWA_KV_W)),
                  pl.BlockSpec((1, tr, SWA_KV_W), lambda bi, i: (bi, i, v_col0 // SWA_KV_W)),
                  pl.BlockSpec((tr, LANES), lambda bi, i: (i, 0)),
                  pl.BlockSpec((tr, LANES), lambda bi, i: (i, 0))],
        out_specs=(pl.BlockSpec((1, tr, SWA_Q_W), lambda bi, i: (bi, i, 0)),
                   pl.BlockSpec((1, 2 * SWA_KV_W, tr), lambda bi, i: (bi, 0, i)),
                   pl.BlockSpec((1, tr, 2 * SWA_KV_W), lambda bi, i: (bi, i, 0))),
        compiler_params=_cparams("arbitrary", "arbitrary"),
        name="swa_prep",
    )(p, p, p, cos_t, sin_t)


def _rope_tables(t_len):
    rows = t_len // GRID_W
    row = jnp.broadcast_to(jnp.arange(rows)[:, None], (rows, GRID_W)).reshape(t_len).astype(F32)
    col = jnp.broadcast_to(jnp.arange(GRID_W)[None, :], (rows, GRID_W)).reshape(t_len).astype(F32)
    n_freq = SWA_HEAD_DIM // 4
    freq = jnp.power(ROPE_THETA, -jnp.arange(n_freq, dtype=F32) / n_freq)
    ang = jnp.concatenate([row[:, None] * freq, col[:, None] * freq], axis=-1)
    cos = jnp.cos(ang)
    sin = jnp.sin(ang)
    reps = LANES // SWA_HEAD_DIM
    return (jnp.tile(jnp.concatenate([cos, cos], axis=-1), (1, reps)),
            jnp.tile(jnp.concatenate([-sin, sin], axis=-1), (1, reps)))


def _attend(q, key_blocks, sink_ref, o_ref):
    n_rows = q.shape[0]
    lane = lax.broadcasted_iota(jnp.int32, (n_rows, LANES), 1)
    lo = lane < LANES // 2
    hpg = SWA_Q_HEADS // SWA_KV_HEADS
    def group_scores(g):
        gs = slice(g * LANES, (g + 1) * LANES)
        q_parts, sink_parts = [], []
        for r in range(hpg):
            head = g * hpg + r
            qc = q[:, (head // 2) * LANES:(head // 2 + 1) * LANES]
            q_parts.append(jnp.where(lo if head % 2 == 0 else jnp.logical_not(lo), qc, jnp.zeros_like(qc)))
            sink_parts.append(jnp.broadcast_to(sink_ref[0:1, head:head + 1], (n_rows, 1)))
        q4 = jnp.concatenate(q_parts, axis=0)
        sink = jnp.concatenate(sink_parts, axis=0)
        scores = []
        m_tile = None
        for kt, _, mask in key_blocks:
            s = _dot(q4, kt[gs, :])
            if mask is not None:
                s = jnp.where(mask, s, NEG_BIG)
            scores.append(s)
            for t in range(s.shape[1] // LANES):
                st = s[:, t * LANES:(t + 1) * LANES]
                m_tile = st if m_tile is None else jnp.maximum(m_tile, st)
        return sink, scores, m_tile

    def group_finish(g, sink, scores, m_tile):
        gs = slice(g * LANES, (g + 1) * LANES)
        m = jnp.maximum(sink, jnp.max(m_tile, axis=-1, keepdims=True))
        acc = jnp.zeros((hpg * n_rows, 2 * LANES), F32)
        for s, (_, v, _) in zip(scores, key_blocks):
            pr = jnp.exp(s - m).astype(BF16)
            v_ones = jnp.concatenate([v[:, gs], jnp.ones((v.shape[0], LANES), BF16)], axis=1)
            acc = acc + _dot(pr, v_ones)
        o = acc[:, :LANES] * (1.0 / (acc[:, LANES:] + jnp.exp(sink - m)))
        for pair in range(hpg // 2):
            col = (g * hpg) // 2 + pair
            even = o[(2 * pair) * n_rows:(2 * pair + 1) * n_rows]
            odd = o[(2 * pair + 1) * n_rows:(2 * pair + 2) * n_rows]
            o_ref[0, :, col * LANES:(col + 1) * LANES] = jnp.where(lo, even, odd).astype(o_ref.dtype)

    pending = group_scores(0)
    for g in range(SWA_KV_HEADS):
        following = group_scores(g + 1) if g + 1 < SWA_KV_HEADS else None
        group_finish(g, *pending)
        pending = following


def _swa_latent_kernel(q_ref, kp_ref, kc_ref, kn_ref, vp_ref, vc_ref, vn_ref, kx_ref, vx_ref, sink_ref, o_ref):
    i = pl.program_id(1)
    nblk = pl.num_programs(1)
    blk = q_ref.shape[1]
    rows4 = (SWA_Q_HEADS // SWA_KV_HEADS) * blk
    qi = lax.broadcasted_iota(jnp.int32, (rows4, blk), 0) & (blk - 1)
    kj = lax.broadcasted_iota(jnp.int32, (rows4, blk), 1)
    mask_prev = kj >= qi + jnp.where(i > 0, 0, blk)
    mask_next = kj <= qi - jnp.where(i < nblk - 1, 0, blk)
    blocks = [(kp_ref[0], vp_ref[0], mask_prev), (kc_ref[0], vc_ref[0], None),
              (kn_ref[0], vn_ref[0], mask_next), (kx_ref[0], vx_ref[0], None)]
    _attend(q_ref[0], blocks, sink_ref, o_ref)


def _swa_latent(q, kt, v2, kt_c, v2_c, sinks):
    bsz, t_len, _ = q.shape
    l_len = v2_c.shape[1]
    blk = SWA_BLOCK
    assert blk & (blk - 1) == 0
    nblk = t_len // blk
    kvw = v2.shape[2]
    prev = lambda i: jnp.maximum(i - 1, 0)
    nxt = lambda i: jnp.minimum(i + 1, nblk - 1)

    def kspec(f):
        return pl.BlockSpec((1, kvw, blk), lambda bi, i: (bi, 0, f(i)))

    def vspec(f):
        return pl.BlockSpec((1, blk, kvw), lambda bi, i: (bi, f(i), 0))

    same = lambda i: i
    return pl.pallas_call(
        _swa_latent_kernel,
        out_shape=jax.ShapeDtypeStruct((bsz, t_len, SWA_Q_W), BF16),
        grid=(bsz, nblk),
        in_specs=[pl.BlockSpec((1, blk, SWA_Q_W), lambda bi, i: (bi, i, 0)),
                  kspec(prev), kspec(same), kspec(nxt), vspec(prev), vspec(same), vspec(nxt),
                  pl.BlockSpec((1, kvw, l_len), lambda bi, i: (bi, 0, 0)),
                  pl.BlockSpec((1, l_len, kvw), lambda bi, i: (bi, 0, 0)),
                  pl.BlockSpec((1, SWA_Q_HEADS), lambda bi, i: (0, 0))],
        out_specs=pl.BlockSpec((1, blk, SWA_Q_W), lambda bi, i: (bi, i, 0)),
        compiler_params=_cparams("arbitrary", "arbitrary"),
        name="swa_latent",
    )(q, kt, kt, kt, v2, v2, v2, kt_c, v2_c, sinks)


def _swa_context_kernel(q_ref, k_ref, v_ref, sink_ref, o_ref):
    _attend(q_ref[0], [(k_ref[0], v_ref[0], None)], sink_ref, o_ref)


def _swa_context(q, kt, v2, sinks):
    bsz, l_len, _ = q.shape
    kvw = v2.shape[2]
    return pl.pallas_call(
        _swa_context_kernel,
        out_shape=jax.ShapeDtypeStruct((bsz, l_len, SWA_Q_W), BF16),
        grid=(bsz,),
        in_specs=[pl.BlockSpec((1, l_len, SWA_Q_W), lambda bi: (bi, 0, 0)),
                  pl.BlockSpec((1, kvw, l_len), lambda bi: (bi, 0, 0)),
                  pl.BlockSpec((1, l_len, kvw), lambda bi: (bi, 0, 0)),
                  pl.BlockSpec((1, SWA_Q_HEADS), lambda bi: (0, 0))],
        out_specs=pl.BlockSpec((1, l_len, SWA_Q_W), lambda bi: (bi, 0, 0)),
        compiler_params=_cparams("arbitrary"),
        name="swa_context",
    )(q, kt, v2, sinks)


def _sgu_kernel(u_ref, v_ref, g_ref, b_ref, ws_ref, bs_ref, o_ref):
    n_rows = u_ref.shape[1]
    v = _layer_norm_rows(_gelu_tanh(v_ref[0].astype(F32)), g_ref[...], b_ref[...]).astype(BF16)
    cw = v.shape[1] // SGU_GROUPS
    for c in range(n_rows // SGU_CHUNK):
        rs = slice(c * SGU_CHUNK, (c + 1) * SGU_CHUNK)
        for g in range(SGU_GROUPS):
            cs = slice(g * cw, (g + 1) * cw)
            s = _dot(ws_ref[g], v[rs, cs]) + bs_ref[:, g:g + 1]
            o_ref[0, rs, cs] = (_gelu_tanh(u_ref[0, rs, cs].astype(F32)) * s).astype(o_ref.dtype)


def _sgu(p, u_col0, v_col0, ln_g, ln_b, ws, bs_t):
    bsz, t_len, _ = p.shape
    w = ln_g.shape[1]
    tr = _pick_tile(t_len, 2 * SGU_CHUNK)
    full = lambda bi, i: (0, 0)
    return pl.pallas_call(
        _sgu_kernel,
        out_shape=jax.ShapeDtypeStruct((bsz, t_len, w), BF16),
        grid=(bsz, t_len // tr),
        in_specs=[pl.BlockSpec((1, tr, w), lambda bi, i: (bi, i, u_col0 // w)),
                  pl.BlockSpec((1, tr, w), lambda bi, i: (bi, i, v_col0 // w)),
                  pl.BlockSpec((1, w), full), pl.BlockSpec((1, w), full),
                  pl.BlockSpec(ws.shape, lambda bi, i: (0, 0, 0)),
                  pl.BlockSpec(bs_t.shape, full)],
        out_specs=pl.BlockSpec((1, tr, w), lambda bi, i: (bi, i, 0)),
        compiler_params=_cparams("arbitrary", "arbitrary"),
        name="sgu",
    )(p, p, ln_g, ln_b, ws, bs_t)


def _merge_kernel(ya_ref, yb_ref, yc_ref, ga_ref, gb_ref, gc_ref, wa_ref, wb_ref, wc_ref, o_ref):
    y = (jax.nn.sigmoid(ga_ref[0].astype(F32)) * _dot(ya_ref[0], wa_ref[...])
         + jax.nn.sigmoid(gb_ref[0].astype(F32)) * _dot(yb_ref[0], wb_ref[...])
         + jax.nn.sigmoid(gc_ref[0].astype(F32)) * _dot(yc_ref[0], wc_ref[...]))
    o_ref[0] = y.astype(o_ref.dtype)


def _merge(ya, yb, yc, p, gate_col0, wa, wb, wc):
    bsz, t_len, _ = ya.shape
    d = wa.shape[1]
    tm = _pick_tile(t_len, 512)
    tn = d
    nb = d // tn
    g0 = gate_col0 // tn

    def yspec(width):
        return pl.BlockSpec((1, tm, width), lambda bi, i, j: (bi, i, 0))

    def gspec(k):
        return pl.BlockSpec((1, tm, tn), lambda bi, i, j: (bi, i, g0 + k * nb + j))

    def wspec(width):
        return pl.BlockSpec((width, tn), lambda bi, i, j: (0, j), pipeline_mode=pl.Buffered(1))

    return pl.pallas_call(
        _merge_kernel,
        out_shape=jax.ShapeDtypeStruct((bsz, t_len, d), BF16),
        grid=(bsz, t_len // tm, nb),
        in_specs=[yspec(ya.shape[2]), yspec(yb.shape[2]), yspec(yc.shape[2]),
                  gspec(0), gspec(1), gspec(2),
                  wspec(wa.shape[0]), wspec(wb.shape[0]), wspec(wc.shape[0])],
        out_specs=pl.BlockSpec((1, tm, tn), lambda bi, i, j: (bi, i, j)),
        compiler_params=_cparams("arbitrary", "arbitrary", "arbitrary"),
        name="merge",
    )(ya, yb, yc, p, p, p, wa, wb, wc)


def _outproj_kernel(y_ref, w_ref, x_ref, mod_ref, g_ref, b_ref, o_ref, *, alpha):
    o = _dot(y_ref[0], w_ref[...])
    r = alpha * x_ref[0] + mod_ref[0, 2:3, :] * o
    o_ref[0] = _layer_norm_rows(r, g_ref[...], b_ref[...])


def _outproj(y, w, x, mod, ln_g, ln_b, alpha):
    bsz, t_len, d = x.shape
    tm = _pick_tile(t_len, 512)
    per_batch = mod.shape[0] > 1
    full = lambda bi, i: (0, 0)
    return pl.pallas_call(
        functools.partial(_outproj_kernel, alpha=alpha),
        out_shape=jax.ShapeDtypeStruct((bsz, t_len, d), F32),
        grid=(bsz, t_len // tm),
        in_specs=[pl.BlockSpec((1, tm, d), lambda bi, i: (bi, i, 0)),
                  pl.BlockSpec((d, d), full, pipeline_mode=pl.Buffered(1)),
                  pl.BlockSpec((1, tm, d), lambda bi, i: (bi, i, 0)),
                  pl.BlockSpec((1, N_MOD, d), (lambda bi, i: (bi, 0, 0)) if per_batch else (lambda bi, i: (0, 0, 0))),
                  pl.BlockSpec((1, d), full), pl.BlockSpec((1, d), full)],
        out_specs=pl.BlockSpec((1, tm, d), lambda bi, i: (bi, i, 0)),
        compiler_params=_cparams("arbitrary", "arbitrary"),
        name="out_proj_ln",
    )(y, w, x, mod, ln_g, ln_b)


def _ffn_kernel(x_ref, mod_ref, w1_ref, w2_ref, g_ref, b_ref, o_ref, h_scr, *, alpha):
    j = pl.program_id(2)

    @pl.when(j == 0)
    def _():
        h_scr[...] = (x_ref[0] * (1.0 + mod_ref[0, 4:5, :]) + mod_ref[0, 3:4, :]).astype(BF16)
        o_ref[...] = jnp.zeros_like(o_ref)

    a = jnp.maximum(_dot(h_scr[...], w1_ref[...]), 0.0)
    o_ref[0] += _dot((a * a).astype(BF16), w2_ref[...])

    @pl.when(j == pl.num_programs(2) - 1)
    def _():
        r = alpha * x_ref[0] + mod_ref[0, 5:6, :] * o_ref[0]
        o_ref[0] = _layer_norm_rows(r, g_ref[...], b_ref[...])


def _ffn(x, mod, w1, w2, ln_g, ln_b, alpha):
    bsz, t_len, d = x.shape
    dff = w1.shape[1]
    tm = _pick_tile(t_len, 1024)
    tf = _pick_tile(dff, 512)
    per_batch = mod.shape[0] > 1
    full = lambda bi, i, j: (0, 0)
    return pl.pallas_call(
        functools.partial(_ffn_kernel, alpha=alpha),
        out_shape=jax.ShapeDtypeStruct((bsz, t_len, d), F32),
        grid=(bsz, t_len // tm, dff // tf),
        in_specs=[pl.BlockSpec((1, tm, d), lambda bi, i, j: (bi, i, 0)),
                  pl.BlockSpec((1, N_MOD, d), (lambda bi, i, j: (bi, 0, 0)) if per_batch else (lambda bi, i, j: (0, 0, 0))),
                  pl.BlockSpec((d, tf), lambda bi, i, j: (0, j)),
                  pl.BlockSpec((tf, d), lambda bi, i, j: (j, 0)),
                  pl.BlockSpec((1, d), full), pl.BlockSpec((1, d), full)],
        out_specs=pl.BlockSpec((1, tm, d), lambda bi, i, j: (bi, i, 0)),
        scratch_shapes=[pltpu.VMEM((tm, d), BF16)],
        compiler_params=_cparams("arbitrary", "arbitrary", "arbitrary"),
        name="ffn_ln",
    )(x, mod, w1, w2, ln_g, ln_b)


def _in_proj_layout(d_model):
    sgu_w = d_model // 2
    src = {}
    off = 0
    for name, width in (("qkv", GDN_QKV_W), ("z", GDN_V_W), ("gdn_gates", 4 * GDN_HEADS),
                        ("swa_q", SWA_Q_W), ("swa_k", SWA_KV_W), ("swa_v", SWA_KV_W),
                        ("sgu_u", sgu_w), ("sgu_v", sgu_w),
                        ("gate_a", d_model), ("gate_b", d_model), ("gate_c", d_model)):
        src[name] = (off, width)
        off += width
    order = ("gate_a", "gate_b", "gate_c", "z", "swa_q", "sgu_u", "sgu_v", "qkv", "swa_k", "swa_v")
    dst = {}
    off = 0
    for name in order:
        dst[name] = off
        assert off % min(src[name][1], 2048) == 0 or name == "qkv"
        off += src[name][1]
    return src, order, dst


def kernel(x, c, ctx, c_ctx, w_ada, b_ada, w_in, b_in, gdn_conv, gdn_a_log, gdn_dt_bias, gdn_norm, swa_sinks, sgu_ln_g, sgu_ln_b, sgu_w, sgu_b, w_branch_a, w_branch_b, w_branch_c, w_out, ln_mix_g, ln_mix_b, w_ff1, w_ff2, ln_ff_g, ln_ff_b):
    bsz, t_len, d = x.shape
    depth = w_ada.shape[0]
    alpha = (2 * depth) ** 0.25
    src, order, dst = _in_proj_layout(d)

    pad_rows = (-(bsz + 1)) % SUBLANES
    cc = jnp.concatenate([c, c_ctx[None, :], jnp.zeros((pad_rows, d), F32)], axis=0)
    mod_all = _ada_mod(cc, w_ada, b_ada).reshape(depth, bsz + 1 + pad_rows, N_MOD, d)
    cos_t, sin_t = _rope_tables(t_len)
    ones_t = jnp.ones((ctx.shape[1], LANES), F32)
    zeros_t = jnp.zeros((ctx.shape[1], LANES), F32)
    masks = _gdn_masks()
    tri = _gdn_tri()
    n_gate = 4 * GDN_HEADS
    lane_pad = LANES - n_gate

    xc = ctx
    for l in range(depth):
        need_ctx = l < depth - 1
        mod_l = mod_all[l, :bsz]
        mod_c = mod_all[l, bsz:bsz + 1]

        def cols(a, name):
            s0, wd = src[name]
            return a[..., s0:s0 + wd]

        w_main = jnp.concatenate([cols(w_in[l], n) for n in order], axis=1).astype(BF16)
        b_main = jnp.concatenate([cols(b_in[l], n) for n in order], axis=0)[None, :]
        w_g = jnp.pad(cols(w_in[l], "gdn_gates"), ((0, 0), (0, lane_pad))).astype(BF16)
        b_g = jnp.pad(cols(b_in[l], "gdn_gates"), (0, lane_pad))[None, :]
        decay_pad = (GATE_RAW_DECAY, LANES - 2 * GATE_RAW_DECAY)
        alog = jnp.pad(gdn_a_log[l].reshape(-1), decay_pad)[None, :]
        dtb = jnp.pad(gdn_dt_bias[l].reshape(-1), decay_pad)[None, :]

        p_l = _inproj(x, mod_l, w_main, b_main)
        p_c = _inproj(xc, mod_c, w_main, b_main)
        tab_l, rows_l = _gdn_gates(x, mod_l, w_g, b_g, alog, dtb, tri)
        tab_c, rows_c = _gdn_gates(xc, mod_c, w_g, b_g, alog, dtb, tri)

        qkv_l = _gdn_conv(p_l, gdn_conv[l], dst["qkv"])
        qkv_c = _gdn_conv(p_c, gdn_conv[l], dst["qkv"])
        ya_l, ya_c = _gdn(qkv_l, qkv_c, tab_l, rows_l, tab_c, rows_c, masks, p_l, p_c, dst["z"],
                          gdn_norm[l][None, :])

        sinks = swa_sinks[l][None, :]
        q_l, kt_l, v2_l = _swa_prep(p_l, dst["swa_q"], dst["swa_k"], dst["swa_v"], cos_t, sin_t, True)
        q_c, kt_c, v2_c = _swa_prep(p_c, dst["swa_q"], dst["swa_k"], dst["swa_v"], ones_t, zeros_t, False)
        yb_l = _swa_latent(q_l, kt_l, v2_l, kt_c, v2_c, sinks)

        ws = sgu_w[l].astype(BF16)
        bs_t = sgu_b[l].T
        sg, sb = sgu_ln_g[l][None, :], sgu_ln_b[l][None, :]
        yc_l = _sgu(p_l, dst["sgu_u"], dst["sgu_v"], sg, sb, ws, bs_t)

        wa, wb, wc = w_branch_a[l].astype(BF16), w_branch_b[l].astype(BF16), w_branch_c[l].astype(BF16)
        wo = w_out[l].astype(BF16)
        w1, w2 = w_ff1[l].astype(BF16), w_ff2[l].astype(BF16)
        lmg, lmb = ln_mix_g[l][None, :], ln_mix_b[l][None, :]
        lfg, lfb = ln_ff_g[l][None, :], ln_ff_b[l][None, :]

        y_l = _merge(ya_l, yb_l, yc_l, p_l, dst["gate_a"], wa, wb, wc)
        x = _outproj(y_l, wo, x, mod_l, lmg, lmb, alpha)
        x = _ffn(x, mod_l, w1, w2, lfg, lfb, alpha)
        if need_ctx:
            yb_c = _swa_context(q_c, kt_c, v2_c, sinks)
            yc_c = _sgu(p_c, dst["sgu_u"], dst["sgu_v"], sg, sb, ws, bs_t)
            y_c = _merge(ya_c, yb_c, yc_c, p_c, dst["gate_a"], wa, wb, wc)
            xc = _outproj(y_c, wo, xc, mod_c, lmg, lmb, alpha)
            xc = _ffn(xc, mod_c, w1, w2, lfg, lfb, alpha)
    return x
```

```python
import functools
import math

import jax
import jax.numpy as jnp
import numpy as np
from jax import lax
from jax.experimental import pallas as pl
from jax.experimental.pallas import tpu as pltpu

F32 = jnp.float32
BF16 = jnp.bfloat16

GRID_W = 64
GDN_HEADS = 8
GDN_DK = 128
GDN_DV = 128
GDN_CONV = 5
SWA_Q_HEADS = 16
SWA_KV_HEADS = 4
SWA_HEAD_DIM = 64
SWA_BLOCK = 128
ROPE_THETA = 10000.0
SGU_GROUPS = 8
SGU_CHUNK = 128
N_MOD = 6
LN_EPS = 1e-5
RMS_EPS = 1e-6

GDN_QK_W = GDN_HEADS * GDN_DK
GDN_V_W = GDN_HEADS * GDN_DV
GDN_QKV_W = 2 * GDN_QK_W + GDN_V_W
SWA_Q_W = SWA_Q_HEADS * SWA_HEAD_DIM
SWA_KV_W = SWA_KV_HEADS * SWA_HEAD_DIM

LANES = 128
SUBLANES = 8
VMEM_LIMIT_BYTES = 56 * 1024 * 1024

GDN_CHUNK = 128
GDN_SERIES_BLOCK = 16
assert GDN_CHUNK == GDN_DK == GDN_DV == LANES
NEG_BIG = -1e30


def _cparams(*sem):
    return pltpu.CompilerParams(dimension_semantics=sem, vmem_limit_bytes=VMEM_LIMIT_BYTES)


def _dot(a, b):
    return jnp.dot(a, b, preferred_element_type=F32)


def _dot_nt(a, b):
    return lax.dot_general(a, b, (((1,), (1,)), ((), ())), preferred_element_type=F32)


def _layer_norm_rows(r, g, b):
    mu = jnp.mean(r, axis=-1, keepdims=True)
    d = r - mu
    var = jnp.mean(d * d, axis=-1, keepdims=True)
    return d * lax.rsqrt(var + LN_EPS) * g + b


def _gelu_tanh(x):
    return x * (0.5 * (1.0 + jnp.tanh(math.sqrt(2.0 / math.pi) * (x + 0.044715 * (x * x * x)))))


def _pick_tile(n, pref):
    t = min(n, pref)
    while n % t:
        t //= 2
    return t


def _ada_kernel(c_ref, w_ref, b_ref, o_ref):
    c = c_ref[...]
    a = (c * jax.nn.sigmoid(c)).astype(BF16)
    o_ref[0] = _dot(a, w_ref[0].astype(BF16)) + b_ref[0]


def _ada_mod(cc, w_ada, b_ada):
    nl, d, n = w_ada.shape
    tn = _pick_tile(n, 1024)
    return pl.pallas_call(
        _ada_kernel,
        out_shape=jax.ShapeDtypeStruct((nl, cc.shape[0], n), F32),
        grid=(nl, n // tn),
        in_specs=[pl.BlockSpec(cc.shape, lambda l, j: (0, 0)),
                  pl.BlockSpec((1, d, tn), lambda l, j: (l, 0, j)),
                  pl.BlockSpec((1, 1, tn), lambda l, j: (l, 0, j))],
        out_specs=pl.BlockSpec((1, cc.shape[0], tn), lambda l, j: (l, 0, j)),
        compiler_params=_cparams("arbitrary", "arbitrary"),
        name="ada_mod",
    )(cc, w_ada, b_ada.reshape(nl, 1, n))


def _inproj_kernel(x_ref, mod_ref, w_ref, b_ref, o_ref, h_scr):
    @pl.when(pl.program_id(2) == 0)
    def _():
        shift = mod_ref[0, 0:1, :]
        scale = mod_ref[0, 1:2, :]
        h_scr[...] = (x_ref[0] * (1.0 + scale) + shift).astype(BF16)

    o_ref[0] = (_dot(h_scr[...], w_ref[...]) + b_ref[...]).astype(o_ref.dtype)


def _inproj(x, mod, w, b):
    bsz, t_len, d = x.shape
    n = w.shape[1]
    tm = _pick_tile(t_len, 1024)
    tn = _pick_tile(n, 1536)
    per_batch = mod.shape[0] > 1
    return pl.pallas_call(
        _inproj_kernel,
        out_shape=jax.ShapeDtypeStruct((bsz, t_len, n), BF16),
        grid=(bsz, t_len // tm, n // tn),
        in_specs=[pl.BlockSpec((1, tm, d), lambda bi, i, j: (bi, i, 0)),
                  pl.BlockSpec((1, N_MOD, d), (lambda bi, i, j: (bi, 0, 0)) if per_batch else (lambda bi, i, j: (0, 0, 0))),
                  pl.BlockSpec((d, tn), lambda bi, i, j: (0, j)),
                  pl.BlockSpec((1, tn), lambda bi, i, j: (0, j))],
        out_specs=pl.BlockSpec((1, tm, tn), lambda bi, i, j: (bi, i, j)),
        scratch_shapes=[pltpu.VMEM((tm, d), BF16)],
        compiler_params=_cparams("arbitrary", "arbitrary", "arbitrary"),
        name="in_proj",
    )(x, mod, w, b)


GATE_BETA, GATE_CUM, GATE_E, GATE_EK, GATE_ETOT = 0, 16, 32, 48, 64
DIR_STRIDE = GDN_HEADS
GATE_RAW_DECAY = 2 * GDN_HEADS


def _split3_bf16(x):
    hi = x.astype(BF16)
    r1 = x - hi.astype(F32)
    mid = r1.astype(BF16)
    lo = (r1 - mid.astype(F32)).astype(BF16)
    return hi, mid, lo


def _gdn_gate_kernel(x_ref, mod_ref, w_ref, b_ref, alog_ref, dtb_ref, tri_ref, cols_ref, rows_ref):
    c_len = GDN_CHUNK
    tm = x_ref.shape[1]
    shift = mod_ref[0, 0:1, :]
    scale = mod_ref[0, 1:2, :]
    h = (x_ref[0] * (1.0 + scale) + shift).astype(BF16)
    st = _dot(h, w_ref[...]) + b_ref[...]
    lane = lax.broadcasted_iota(jnp.int32, (c_len, LANES), 1)
    is_decay = (lane >= GATE_RAW_DECAY) & (lane < 2 * GATE_RAW_DECAY)
    tri = tri_ref[...]
    for c in range(tm // c_len):
        rs = slice(c * c_len, (c + 1) * c_len)
        stc = st[rs]
        beta = jax.nn.sigmoid(stc)
        xs = stc + dtb_ref[...]
        softplus = jnp.maximum(xs, 0.0) + jnp.log1p(jnp.exp(-jnp.abs(xs)))
        g = jnp.where(is_decay, -jnp.exp(alog_ref[...]) * softplus, 0.0)
        t3 = sum(_dot(tri, piece) for piece in _split3_bf16(g))
        tot = t3[2 * c_len:3 * c_len]
        cum = jnp.where(lane >= GATE_RAW_DECAY + DIR_STRIDE, t3[c_len:2 * c_len], t3[0:c_len])
        e = jnp.exp(cum)
        ek = jnp.exp(tot - cum)
        et = jnp.exp(tot)
        assert GATE_CUM == GATE_RAW_DECAY
        table = jnp.where(lane < GATE_CUM, beta,
                          jnp.where(lane < GATE_E, cum,
                                    jnp.where(lane < GATE_EK, pltpu.roll(e, GATE_E - GATE_RAW_DECAY, axis=1),
                                              jnp.where(lane < GATE_ETOT, pltpu.roll(ek, GATE_EK - GATE_RAW_DECAY, axis=1),
                                                        pltpu.roll(et, GATE_ETOT - GATE_RAW_DECAY, axis=1)))))
        for hh in range(GDN_HEADS):
            cols_ref[0, hh, rs, :] = pltpu.roll(table, LANES - hh, axis=1) if hh else table
        rows_ref[0, :, rs] = cum.T[GATE_RAW_DECAY:2 * GATE_RAW_DECAY]


def _gdn_gates(x, mod, w, b, alog, dtb, tri):
    bsz, t_len, d = x.shape
    tm = _pick_tile(t_len, 512)
    per_batch = mod.shape[0] > 1
    small = lambda bi, i: (0, 0)
    return pl.pallas_call(
        _gdn_gate_kernel,
        out_shape=(jax.ShapeDtypeStruct((bsz, GDN_HEADS, t_len, LANES), F32),
                   jax.ShapeDtypeStruct((bsz, 2 * GDN_HEADS, t_len), F32)),
        grid=(bsz, t_len // tm),
        in_specs=[pl.BlockSpec((1, tm, d), lambda bi, i: (bi, i, 0)),
                  pl.BlockSpec((1, N_MOD, d), (lambda bi, i: (bi, 0, 0)) if per_batch else (lambda bi, i: (0, 0, 0))),
                  pl.BlockSpec((d, LANES), small),
                  pl.BlockSpec((1, LANES), small),
                  pl.BlockSpec((1, LANES), small),
                  pl.BlockSpec((1, LANES), small),
                  pl.BlockSpec(tri.shape, small)],
        out_specs=(pl.BlockSpec((1, GDN_HEADS, tm, LANES), lambda bi, i: (bi, 0, i, 0)),
                   pl.BlockSpec((1, 2 * GDN_HEADS, tm), lambda bi, i: (bi, 0, i))),
        compiler_params=_cparams("arbitrary", "arbitrary"),
        name="gdn_gates",
    )(x, mod, w, b, alog, dtb, tri)


CONV_ROWS = 256
CONV_HALO = SUBLANES


def _gdn_conv_kernel(x_ref, w_ref, o_ref, pad_scr):
    t_len = x_ref.shape[1]
    j = pl.program_id(1)
    zeros = jnp.zeros((CONV_HALO, LANES), F32)
    pad_scr[0:CONV_HALO, :] = zeros
    pad_scr[t_len + CONV_HALO:t_len + 2 * CONV_HALO, :] = zeros
    pad_scr[CONV_HALO:t_len + CONV_HALO, :] = x_ref[0].astype(F32)
    heads_qk = 2 * GDN_HEADS
    qk_scale = jnp.where(j < GDN_HEADS, GDN_DK ** -0.5, 1.0).astype(F32)
    w = w_ref[...]
    half = GDN_CONV // 2

    def conv_silu(c):
        r0 = pl.multiple_of(c * CONV_ROWS, CONV_ROWS)
        acc = jnp.zeros((CONV_ROWS, LANES), F32)
        for k in range(GDN_CONV):
            off = CONV_HALO - half + k
            acc = acc + pad_scr[pl.ds(r0 + off, CONV_ROWS), :] * w[k:k + 1, :]
        return r0, acc * jax.nn.sigmoid(acc)

    def body_qk(c, carry):
        r0, y = conv_silu(c)
        ss = jnp.sum(y * y, axis=-1, keepdims=True)
        o_ref[0, pl.ds(r0, CONV_ROWS), :] = (y * (lax.rsqrt(ss + RMS_EPS) * qk_scale)).astype(o_ref.dtype)
        return carry

    def body_v(c, carry):
        r0, y = conv_silu(c)
        o_ref[0, pl.ds(r0, CONV_ROWS), :] = y.astype(o_ref.dtype)
        return carry

    @pl.when(j < heads_qk)
    def _():
        lax.fori_loop(0, t_len // CONV_ROWS, body_qk, 0, unroll=2 if t_len // CONV_ROWS % 2 == 0 else 1)

    @pl.when(j >= heads_qk)
    def _():
        lax.fori_loop(0, t_len // CONV_ROWS, body_v, 0)


def _gdn_conv(p, conv_w, col0):
    bsz, t_len, _ = p.shape
    nblk = GDN_QKV_W // LANES
    cb = col0 // LANES
    return pl.pallas_call(
        _gdn_conv_kernel,
        out_shape=jax.ShapeDtypeStruct((bsz, t_len, GDN_QKV_W), BF16),
        grid=(bsz, nblk),
        in_specs=[pl.BlockSpec((1, t_len, LANES), lambda bi, j: (bi, 0, cb + j)),
                  pl.BlockSpec((GDN_CONV, LANES), lambda bi, j: (0, j))],
        out_specs=pl.BlockSpec((1, t_len, LANES), lambda bi, j: (bi, 0, j)),
        scratch_shapes=[pltpu.VMEM((t_len + 2 * CONV_HALO, LANES), F32)],
        compiler_params=_cparams("arbitrary", "arbitrary"),
        name="gdn_conv",
    )(p, conv_w)


M_EYE, M_INCL, M_NEG_STRICT, M_DIAG_BLOCK, M_MERGE = 0, 1, 3, 5, 6
GDN_MERGE_LEVELS = int(math.log2(GDN_CHUNK // GDN_SERIES_BLOCK))
GDN_WIDE_STEPS = int(math.log2(GDN_SERIES_BLOCK)) - 2
GDN_TILE = 1024
GATED_NORM_ROWS = 512


def _gdn_masks():
    c = GDN_CHUNK
    i = np.arange(c)[:, None]
    j = np.arange(c)[None, :]
    rows = [i == j, j <= i, j >= i, -(j < i).astype(np.float32), -(j > i).astype(np.float32),
            (i // GDN_SERIES_BLOCK) == (j // GDN_SERIES_BLOCK)]
    s = GDN_SERIES_BLOCK
    while s < c:
        rows.append(((i // (2 * s)) == (j // (2 * s))) & ((i // s) != (j // s)))
        s *= 2
    return jnp.asarray(np.stack([np.asarray(r, np.float32) for r in rows]))


def _gdn_tri():
    c = GDN_CHUNK
    i = np.arange(c)[:, None]
    j = np.arange(c)[None, :]
    return jnp.asarray(np.concatenate([j <= i, j >= i, np.ones((c, c), bool)], axis=0).astype(np.float32), BF16)


def _in_two_halves(fn, *lists):
    half = (len(lists[0]) + 1) // 2
    out = [fn(*args) for args in zip(*(l[:half] for l in lists))]
    yield
    out += [fn(*args) for args in zip(*(l[half:] for l in lists))]
    yield
    return out


def _gdn_intra_stages(chains_in, m_ref, refs, slot):
    c = GDN_CHUNK
    u_ref, wq_ref, akt_ref, et_ref = refs
    gqs = yield from _in_two_halves(lambda ch: _dot_nt(jnp.concatenate([ch[1], ch[0]], axis=0), ch[1]),
                                    chains_in)
    eye = m_ref[M_EYE]
    chains = []
    for (q, k, v, table, cum_row, d, idx), gq in zip(chains_in, gqs):
        def col(base, table=table, d=d):
            lane = base + d * DIR_STRIDE
            return table[:, lane:lane + 1]

        beta_c, cum_c = col(GATE_BETA), col(GATE_CUM)
        incl = m_ref[M_INCL + d]
        ex = jnp.exp((cum_c - cum_row) * incl)
        neg_m = (gq[:c] * beta_c) * (ex * m_ref[M_NEG_STRICT + d])
        chains.append(dict(q=q, k=k, v=v, table=table, idx=idx, d=d, col=col, beta_c=beta_c,
                           ex_incl=ex * incl, qk=gq[c:], neg_m=neg_m))
    ps = [ch["neg_m"] * m_ref[M_DIAG_BLOCK] for ch in chains]
    sums = [eye + p for p in ps]
    pbs = [p.astype(BF16) for p in ps]
    ps = yield from _in_two_halves(lambda pb: _dot(pb, pb), pbs)
    for _ in range(GDN_WIDE_STEPS):
        pbs = [p.astype(BF16) for p in ps]
        boths = yield from _in_two_halves(
            lambda pb, s: _dot(pb, jnp.concatenate([pb, s.astype(BF16)], axis=1)), pbs, sums)
        ps = [both[:, :c] for both in boths]
        sums = [s + both[:, c:] for s, both in zip(sums, boths)]
    invs = yield from _in_two_halves(lambda p, s: s + _dot(p.astype(BF16), s.astype(BF16)), ps, sums)
    for lvl in range(GDN_MERGE_LEVELS):
        ibs = [inv.astype(BF16) for inv in invs]
        offs = [(ch["neg_m"] * m_ref[M_MERGE + lvl]).astype(BF16) for ch in chains]
        halves = yield from _in_two_halves(lambda ib, off: _dot(ib, off).astype(BF16), ibs, offs)
        invs = yield from _in_two_halves(lambda inv, half, ib: inv + _dot(half, ib), invs, halves, ibs)
    rhss = []
    for ch in chains:
        kf = ch["k"].astype(F32)
        vf = ch["v"].astype(F32)
        e_c = ch["col"](GATE_E)
        rhss.append(jnp.concatenate([(vf * ch["beta_c"]).astype(BF16), (kf * (ch["beta_c"] * e_c)).astype(BF16)], axis=1))
    uws = yield from _in_two_halves(lambda inv, rhs: _dot(inv.astype(BF16), rhs), invs, rhss)
    for ch, uw in zip(chains, uws):
        d, idx = ch["d"], ch["idx"]
        a = (ch["qk"] * ch["ex_incl"]).astype(BF16)
        qd = (ch["q"].astype(F32) * ch["col"](GATE_E)).astype(BF16)
        kdt = (ch["k"].astype(F32) * ch["col"](GATE_EK)).T.astype(BF16)
        u_ref[slot, d, idx] = uw[:, :c].astype(BF16)
        wq_ref[slot, d, idx] = jnp.concatenate([uw[:, c:].astype(BF16), qd], axis=0)
        akt_ref[slot, d, idx] = jnp.concatenate([a, kdt], axis=0)
        lane = GATE_ETOT + d * DIR_STRIDE
        et_ref[slot, d, idx] = jnp.broadcast_to(ch["table"][0:1, lane:lane + 1], (SUBLANES, LANES))


GDN_INTRA_STAGES = 2 * (3 + GDN_WIDE_STEPS + 2 * GDN_MERGE_LEVELS + 1)


def _gdn_scan_stages(refs, slot, s_scrs, o_scr, first_chunks, n):
    c = GDN_CHUNK
    u_ref, wq_ref, akt_ref, et_ref = refs
    for j in range(n):
        idxs = (j, n - 1 - j)
        ss = [s_scrs[d][...] for d in range(2)]
        ws_qs = [_dot(wq_ref[slot, d, idxs[d]], ss[d].astype(BF16)) for d in range(2)]
        yield
        v_news = [(u_ref[slot, d, idxs[d]].astype(F32) - ws_qs[d][:c]).astype(BF16) for d in range(2)]
        av_kvs = [_dot(akt_ref[slot, d, idxs[d]], v_news[d]) for d in range(2)]
        yield
        for d in range(2):
            r0 = pl.multiple_of((first_chunks[d] + idxs[d]) * c, c)
            o_scr[pl.ds(r0, c), :] += ws_qs[d][c:] + av_kvs[d][:c]
            s_scrs[d][...] = ss[d] * et_ref[slot, d, idxs[d]][0:1, :] + av_kvs[d][c:]


def _drain(stages):
    for _ in stages:
        pass


def _interleave(main, side, n_main, n_side):
    done = 0
    for k, _ in enumerate(main, 1):
        target = min(n_side, -(-k * n_side // n_main))
        while done < target:
            next(side)
            done += 1
    _drain(side)


def _gated_norm(o_scr, z_ref, nw, y_ref):
    n_rows = o_scr.shape[0]
    rows = min(GATED_NORM_ROWS, n_rows)

    def body(c, carry):
        r0 = pl.multiple_of(c * rows, rows)
        o = o_scr[pl.ds(r0, rows), :]
        z = z_ref[0, pl.ds(r0, rows), :].astype(F32)
        on = o * lax.rsqrt(jnp.mean(o * o, axis=-1, keepdims=True) + RMS_EPS) * nw
        y_ref[0, pl.ds(r0, rows), :] = (on * (z * jax.nn.sigmoid(z))).astype(y_ref.dtype)
        return carry

    lax.fori_loop(0, n_rows // rows, body, 0)


def _gdn_kernel(qf_ref, kf_ref, vf_ref, tf_ref, rf_ref, qb_ref, kb_ref, vb_ref, tb_ref, rb_ref,
                qc_ref, kc_ref, vc_ref, tc_ref, rfc_ref, rbc_ref,
                m_ref, zl_ref, zc_ref, nw_ref, yl_ref, yc_ref,
                ul, wql, aktl, etl, uc, wqc, aktc, etc, ol_scr, oc_scr, sf_scr, sb_scr):
    c = GDN_CHUNK
    i = pl.program_id(2)
    n_tiles = pl.num_programs(2) - 1
    cpt = qf_ref.shape[1] // c
    n_ctx = qc_ref.shape[1] // c
    s_scrs = (sf_scr, sb_scr)
    lat_refs = (ul, wql, aktl, etl)
    ctx_refs = (uc, wqc, aktc, etc)
    cur = lax.rem(i, 2)
    prev = 1 - cur

    def chains_of(dirs_refs, n):
        chains = []
        for d, (q_ref, k_ref, v_ref, t_ref, r_ref) in dirs_refs:
            for j in range(n):
                rs = slice(j * c, (j + 1) * c)
                chains.append((q_ref[0, rs, :], k_ref[0, rs, :], v_ref[0, rs, :], t_ref[0, 0, rs, :],
                               r_ref[0, 0, j:j + 1, :], d, j))
        return chains

    def intra(slot):
        dirs_refs = ((0, (qf_ref, kf_ref, vf_ref, tf_ref, rf_ref)), (1, (qb_ref, kb_ref, vb_ref, tb_ref, rb_ref)))
        return _gdn_intra_stages(chains_of(dirs_refs, cpt), m_ref, lat_refs, slot)

    def scan(slot, step):
        first_chunks = (step * cpt, (n_tiles - 1 - step) * cpt)
        return _gdn_scan_stages(lat_refs, slot, s_scrs, ol_scr, first_chunks, cpt)

    @pl.when(i == 0)
    def _():
        sf_scr[...] = jnp.zeros_like(sf_scr)
        sb_scr[...] = jnp.zeros_like(sb_scr)
        oc_scr[...] = jnp.zeros_like(oc_scr)
        ol_scr[...] = jnp.zeros_like(ol_scr)
        dirs_refs = ((0, (qc_ref, kc_ref, vc_ref, tc_ref, rfc_ref)), (1, (qc_ref, kc_ref, vc_ref, tc_ref, rbc_ref)))

        def context():
            yield from _gdn_intra_stages(chains_of(dirs_refs, n_ctx), m_ref, ctx_refs, 0)
            yield from _gdn_scan_stages(ctx_refs, 0, s_scrs, oc_scr, (0, 0), n_ctx)

        _interleave(intra(cur), context(), GDN_INTRA_STAGES, GDN_INTRA_STAGES + 2 * n_ctx)

    @pl.when((i > 0) & (i < n_tiles))
    def _():
        _interleave(intra(cur), scan(prev, i - 1), GDN_INTRA_STAGES, 2 * cpt)

    @pl.when(i == n_tiles)
    def _():
        _drain(scan(prev, i - 1))
        nw = nw_ref[...]
        _gated_norm(ol_scr, zl_ref, nw, yl_ref)
        _gated_norm(oc_scr, zc_ref, nw, yc_ref)


def _gdn(qkv_l, qkv_c, tab_l, rows_l, tab_c, rows_c, masks, p_l, p_c, z_col0, norm_w):
    bsz, t_len, _ = qkv_l.shape
    l_len = qkv_c.shape[1]
    c = GDN_CHUNK
    nh = GDN_HEADS
    tb = _pick_tile(t_len, GDN_TILE)
    n_tiles = t_len // tb
    n_lat, n_ctx = t_len // c, l_len // c
    rows_l = rows_l.reshape(bsz, 2 * nh, n_lat, c)
    rows_c = rows_c.reshape(bsz, 2 * nh, n_ctx, c)
    zb = z_col0 // LANES
    tile_of = (lambda i: jnp.minimum(i, n_tiles - 1), lambda i: jnp.maximum(n_tiles - 1 - i, 0))

    def dir_specs(d):
        tile = tile_of[d]
        return ([pl.BlockSpec((1, tb, LANES), lambda bi, h, i, off=off: (bi, tile(i), off + h))
                 for off in (0, nh, 2 * nh)]
                + [pl.BlockSpec((1, 1, tb, LANES), lambda bi, h, i: (bi, h, tile(i), 0)),
                   pl.BlockSpec((1, 1, tb // c, c), lambda bi, h, i: (bi, d * nh + h, tile(i), 0))])

    def ctx_spec(col_off):
        return pl.BlockSpec((1, l_len, LANES), lambda bi, h, i: (bi, 0, col_off + h))

    def full_spec(n_rows, col_off):
        return pl.BlockSpec((1, n_rows, LANES), lambda bi, h, i: (bi, 0, col_off + h))

    in_specs = (dir_specs(0) + dir_specs(1)
                + [ctx_spec(0), ctx_spec(nh), ctx_spec(2 * nh),
                   pl.BlockSpec((1, 1, l_len, LANES), lambda bi, h, i: (bi, h, 0, 0)),
                   pl.BlockSpec((1, 1, n_ctx, c), lambda bi, h, i: (bi, h, 0, 0)),
                   pl.BlockSpec((1, 1, n_ctx, c), lambda bi, h, i: (bi, nh + h, 0, 0)),
                   pl.BlockSpec(masks.shape, lambda bi, h, i: (0, 0, 0)),
                   full_spec(t_len, zb), full_spec(l_len, zb),
                   pl.BlockSpec((1, LANES), lambda bi, h, i: (0, 0))])

    def scratch(n_slots, n_chunk):
        lead = (n_slots, 2, n_chunk)
        return [pltpu.VMEM(lead + (c, GDN_DV), BF16), pltpu.VMEM(lead + (2 * c, GDN_DK), BF16),
                pltpu.VMEM(lead + (2 * c, c), BF16), pltpu.VMEM(lead + (SUBLANES, LANES), F32)]

    lat = (qkv_l, qkv_l, qkv_l, tab_l, rows_l)
    return pl.pallas_call(
        _gdn_kernel,
        out_shape=(jax.ShapeDtypeStruct((bsz, t_len, GDN_V_W), BF16),
                   jax.ShapeDtypeStruct((bsz, l_len, GDN_V_W), BF16)),
        grid=(bsz, nh, n_tiles + 1),
        in_specs=in_specs,
        out_specs=(full_spec(t_len, 0), full_spec(l_len, 0)),
        scratch_shapes=scratch(2, tb // c) + scratch(1, n_ctx) + [
            pltpu.VMEM((t_len, GDN_DV), F32), pltpu.VMEM((l_len, GDN_DV), F32),
            pltpu.VMEM((GDN_DK, GDN_DV), F32), pltpu.VMEM((GDN_DK, GDN_DV), F32)],
        compiler_params=_cparams("arbitrary", "arbitrary", "arbitrary"),
        name="gdn_scan",
    )(*lat, *lat, qkv_c, qkv_c, qkv_c, tab_c, rows_c, rows_c, masks, p_l, p_c, norm_w)


def _dup_groups(x):
    lane = lax.broadcasted_iota(jnp.int32, (x.shape[0], LANES), 1)
    lo = lane < LANES // 2
    outs = []
    for c in range(x.shape[1] // LANES):
        xc = x[:, c * LANES:(c + 1) * LANES]
        xs = pltpu.roll(xc, LANES // 2, axis=1)
        outs.append(jnp.where(lo, xc, xs))
        outs.append(jnp.where(lo, xs, xc))
    return outs


def _rope_kernel(q_ref, k_ref, v_ref, cos_ref, sin_ref, qo_ref, kt_ref, vo_ref, *, rope):
    q = q_ref[0].astype(F32)
    k = k_ref[0].astype(F32)
    if rope:
        cos = cos_ref[...]
        sin = sin_ref[...]
        quarter = SWA_HEAD_DIM // 2
        lane = lax.broadcasted_iota(jnp.int32, cos.shape, 1)
        first = (lane % SWA_HEAD_DIM) < quarter

        def rot(x):
            outs = []
            for c in range(x.shape[1] // LANES):
                xc = x[:, c * LANES:(c + 1) * LANES]
                partner = jnp.where(first, pltpu.roll(xc, LANES - quarter, axis=1), pltpu.roll(xc, quarter, axis=1))
                outs.append(xc * cos + partner * sin)
            return jnp.concatenate(outs, axis=1)

        q = rot(q)
        k = rot(k)
    qo_ref[0] = (q * (SWA_HEAD_DIM ** -0.5)).astype(qo_ref.dtype)
    for g, kg in enumerate(_dup_groups(k)):
        kt_ref[0, g * LANES:(g + 1) * LANES, :] = kg.T.astype(kt_ref.dtype)
    vo_ref[0] = jnp.concatenate(_dup_groups(v_ref[0].astype(F32)), axis=1).astype(vo_ref.dtype)


def _swa_prep(p, q_col0, k_col0, v_col0, cos_t, sin_t, rope):
    bsz, t_len, _ = p.shape
    tr = _pick_tile(t_len, 512)
    return pl.pallas_call(
        functools.partial(_rope_kernel, rope=rope),
        out_shape=(jax.ShapeDtypeStruct((bsz, t_len, SWA_Q_W), BF16),
                   jax.ShapeDtypeStruct((bsz, 2 * SWA_KV_W, t_len), BF16),
                   jax.ShapeDtypeStruct((bsz, t_len, 2 * SWA_KV_W), BF16)),
        grid=(bsz, t_len // tr),
        in_specs=[pl.BlockSpec((1, tr, SWA_Q_W), lambda bi, i: (bi, i, q_col0 // SWA_Q_W)),
                  pl.BlockSpec((1, tr, SWA_KV_W), lambda bi, i: (bi, i, k_col0 // SWA_KV_W)),
                  pl.BlockSpec((1, tr, SWA_KV_W), lambda bi, i: (bi, i, v_col0 // SWA_KV_W)),
                  pl.BlockSpec((tr, LANES), lambda bi, i: (i, 0)),
                  pl.BlockSpec((tr, LANES), lambda bi, i: (i, 0))],
        out_specs=(pl.BlockSpec((1, tr, SWA_Q_W), lambda bi, i: (bi, i, 0)),
                   pl.BlockSpec((1, 2 * SWA_KV_W, tr), lambda bi, i: (bi, 0, i)),
                   pl.BlockSpec((1, tr, 2 * SWA_KV_W), lambda bi, i: (bi, i, 0))),
        compiler_params=_cparams("arbitrary", "arbitrary"),
        name="swa_prep",
    )(p, p, p, cos_t, sin_t)


def _rope_tables(t_len):
    rows = t_len // GRID_W
    row = jnp.broadcast_to(jnp.arange(rows)[:, None], (rows, GRID_W)).reshape(t_len).astype(F32)
    col = jnp.broadcast_to(jnp.arange(GRID_W)[None, :], (rows, GRID_W)).reshape(t_len).astype(F32)
    n_freq = SWA_HEAD_DIM // 4
    freq = jnp.power(ROPE_THETA, -jnp.arange(n_freq, dtype=F32) / n_freq)
    ang = jnp.concatenate([row[:, None] * freq, col[:, None] * freq], axis=-1)
    cos = jnp.cos(ang)
    sin = jnp.sin(ang)
    reps = LANES // SWA_HEAD_DIM
    return (jnp.tile(jnp.concatenate([cos, cos], axis=-1), (1, reps)),
            jnp.tile(jnp.concatenate([-sin, sin], axis=-1), (1, reps)))


def _attend(q, key_blocks, sink_ref, o_ref):
    n_rows = q.shape[0]
    lane = lax.broadcasted_iota(jnp.int32, (n_rows, LANES), 1)
    lo = lane < LANES // 2
    hpg = SWA_Q_HEADS // SWA_KV_HEADS
    def group_scores(g):
        gs = slice(g * LANES, (g + 1) * LANES)
        q_parts, sink_parts = [], []
        for r in range(hpg):
            head = g * hpg + r
            qc = q[:, (head // 2) * LANES:(head // 2 + 1) * LANES]
            q_parts.append(jnp.where(lo if head % 2 == 0 else jnp.logical_not(lo), qc, jnp.zeros_like(qc)))
            sink_parts.append(jnp.broadcast_to(sink_ref[0:1, head:head + 1], (n_rows, 1)))
        q4 = jnp.concatenate(q_parts, axis=0)
        sink = jnp.concatenate(sink_parts, axis=0)
        scores = []
        m_tile = None
        for kt, _, mask in key_blocks:
            s = _dot(q4, kt[gs, :])
            if mask is not None:
                s = jnp.where(mask, s, NEG_BIG)
            scores.append(s)
            for t in range(s.shape[1] // LANES):
                st = s[:, t * LANES:(t + 1) * LANES]
                m_tile = st if m_tile is None else jnp.maximum(m_tile, st)
        return sink, scores, m_tile

    def group_finish(g, sink, scores, m_tile):
        gs = slice(g * LANES, (g + 1) * LANES)
        m = jnp.maximum(sink, jnp.max(m_tile, axis=-1, keepdims=True))
        acc = jnp.zeros((hpg * n_rows, 2 * LANES), F32)
        for s, (_, v, _) in zip(scores, key_blocks):
            pr = jnp.exp(s - m).astype(BF16)
            v_ones = jnp.concatenate([v[:, gs], jnp.ones((v.shape[0], LANES), BF16)], axis=1)
            acc = acc + _dot(pr, v_ones)
        o = acc[:, :LANES] * (1.0 / (acc[:, LANES:] + jnp.exp(sink - m)))
        for pair in range(hpg // 2):
            col = (g * hpg) // 2 + pair
            even = o[(2 * pair) * n_rows:(2 * pair + 1) * n_rows]
            odd = o[(2 * pair + 1) * n_rows:(2 * pair + 2) * n_rows]
            o_ref[0, :, col * LANES:(col + 1) * LANES] = jnp.where(lo, even, odd).astype(o_ref.dtype)

    pending = group_scores(0)
    for g in range(SWA_KV_HEADS):
        following = group_scores(g + 1) if g + 1 < SWA_KV_HEADS else None
        group_finish(g, *pending)
        pending = following


def _swa_latent_kernel(q_ref, kp_ref, kc_ref, kn_ref, vp_ref, vc_ref, vn_ref, kx_ref, vx_ref, sink_ref, o_ref):
    i = pl.program_id(1)
    nblk = pl.num_programs(1)
    blk = q_ref.shape[1]
    rows4 = (SWA_Q_HEADS // SWA_KV_HEADS) * blk
    qi = lax.broadcasted_iota(jnp.int32, (rows4, blk), 0) & (blk - 1)
    kj = lax.broadcasted_iota(jnp.int32, (rows4, blk), 1)
    mask_prev = kj >= qi + jnp.where(i > 0, 0, blk)
    mask_next = kj <= qi - jnp.where(i < nblk - 1, 0, blk)
    blocks = [(kp_ref[0], vp_ref[0], mask_prev), (kc_ref[0], vc_ref[0], None),
              (kn_ref[0], vn_ref[0], mask_next), (kx_ref[0], vx_ref[0], None)]
    _attend(q_ref[0], blocks, sink_ref, o_ref)


def _swa_latent(q, kt, v2, kt_c, v2_c, sinks):
    bsz, t_len, _ = q.shape
    l_len = v2_c.shape[1]
    blk = SWA_BLOCK
    assert blk & (blk - 1) == 0
    nblk = t_len // blk
    kvw = v2.shape[2]
    prev = lambda i: jnp.maximum(i - 1, 0)
    nxt = lambda i: jnp.minimum(i + 1, nblk - 1)

    def kspec(f):
        return pl.BlockSpec((1, kvw, blk), lambda bi, i: (bi, 0, f(i)))

    def vspec(f):
        return pl.BlockSpec((1, blk, kvw), lambda bi, i: (bi, f(i), 0))

    same = lambda i: i
    return pl.pallas_call(
        _swa_latent_kernel,
        out_shape=jax.ShapeDtypeStruct((bsz, t_len, SWA_Q_W), BF16),
        grid=(bsz, nblk),
        in_specs=[pl.BlockSpec((1, blk, SWA_Q_W), lambda bi, i: (bi, i, 0)),
                  kspec(prev), kspec(same), kspec(nxt), vspec(prev), vspec(same), vspec(nxt),
                  pl.BlockSpec((1, kvw, l_len), lambda bi, i: (bi, 0, 0)),
                  pl.BlockSpec((1, l_len, kvw), lambda bi, i: (bi, 0, 0)),
                  pl.BlockSpec((1, SWA_Q_HEADS), lambda bi, i: (0, 0))],
        out_specs=pl.BlockSpec((1, blk, SWA_Q_W), lambda bi, i: (bi, i, 0)),
        compiler_params=_cparams("arbitrary", "arbitrary"),
        name="swa_latent",
    )(q, kt, kt, kt, v2, v2, v2, kt_c, v2_c, sinks)


def _swa_context_kernel(q_ref, k_ref, v_ref, sink_ref, o_ref):
    _attend(q_ref[0], [(k_ref[0], v_ref[0], None)], sink_ref, o_ref)


def _swa_context(q, kt, v2, sinks):
    bsz, l_len, _ = q.shape
    kvw = v2.shape[2]
    return pl.pallas_call(
        _swa_context_kernel,
        out_shape=jax.ShapeDtypeStruct((bsz, l_len, SWA_Q_W), BF16),
        grid=(bsz,),
        in_specs=[pl.BlockSpec((1, l_len, SWA_Q_W), lambda bi: (bi, 0, 0)),
                  pl.BlockSpec((1, kvw, l_len), lambda bi: (bi, 0, 0)),
                  pl.BlockSpec((1, l_len, kvw), lambda bi: (bi, 0, 0)),
                  pl.BlockSpec((1, SWA_Q_HEADS), lambda bi: (0, 0))],
        out_specs=pl.BlockSpec((1, l_len, SWA_Q_W), lambda bi: (bi, 0, 0)),
        compiler_params=_cparams("arbitrary"),
        name="swa_context",
    )(q, kt, v2, sinks)


def _sgu_kernel(u_ref, v_ref, g_ref, b_ref, ws_ref, bs_ref, o_ref):
    n_rows = u_ref.shape[1]
    v = _layer_norm_rows(_gelu_tanh(v_ref[0].astype(F32)), g_ref[...], b_ref[...]).astype(BF16)
    cw = v.shape[1] // SGU_GROUPS
    for c in range(n_rows // SGU_CHUNK):
        rs = slice(c * SGU_CHUNK, (c + 1) * SGU_CHUNK)
        for g in range(SGU_GROUPS):
            cs = slice(g * cw, (g + 1) * cw)
            s = _dot(ws_ref[g], v[rs, cs]) + bs_ref[:, g:g + 1]
            o_ref[0, rs, cs] = (_gelu_tanh(u_ref[0, rs, cs].astype(F32)) * s).astype(o_ref.dtype)


def _sgu(p, u_col0, v_col0, ln_g, ln_b, ws, bs_t):
    bsz, t_len, _ = p.shape
    w = ln_g.shape[1]
    tr = _pick_tile(t_len, 2 * SGU_CHUNK)
    full = lambda bi, i: (0, 0)
    return pl.pallas_call(
        _sgu_kernel,
        out_shape=jax.ShapeDtypeStruct((bsz, t_len, w), BF16),
        grid=(bsz, t_len // tr),
        in_specs=[pl.BlockSpec((1, tr, w), lambda bi, i: (bi, i, u_col0 // w)),
                  pl.BlockSpec((1, tr, w), lambda bi, i: (bi, i, v_col0 // w)),
                  pl.BlockSpec((1, w), full), pl.BlockSpec((1, w), full),
                  pl.BlockSpec(ws.shape, lambda bi, i: (0, 0, 0)),
                  pl.BlockSpec(bs_t.shape, full)],
        out_specs=pl.BlockSpec((1, tr, w), lambda bi, i: (bi, i, 0)),
        compiler_params=_cparams("arbitrary", "arbitrary"),
        name="sgu",
    )(p, p, ln_g, ln_b, ws, bs_t)


def _merge_kernel(ya_ref, yb_ref, yc_ref, ga_ref, gb_ref, gc_ref, wa_ref, wb_ref, wc_ref, o_ref):
    y = (jax.nn.sigmoid(ga_ref[0].astype(F32)) * _dot(ya_ref[0], wa_ref[...])
         + jax.nn.sigmoid(gb_ref[0].astype(F32)) * _dot(yb_ref[0], wb_ref[...])
         + jax.nn.sigmoid(gc_ref[0].astype(F32)) * _dot(yc_ref[0], wc_ref[...]))
    o_ref[0] = y.astype(o_ref.dtype)


def _merge(ya, yb, yc, p, gate_col0, wa, wb, wc):
    bsz, t_len, _ = ya.shape
    d = wa.shape[1]
    tm = _pick_tile(t_len, 512)
    tn = d
    nb = d // tn
    g0 = gate_col0 // tn

    def yspec(width):
        return pl.BlockSpec((1, tm, width), lambda bi, i, j: (bi, i, 0))

    def gspec(k):
        return pl.BlockSpec((1, tm, tn), lambda bi, i, j: (bi, i, g0 + k * nb + j))

    def wspec(width):
        return pl.BlockSpec((width, tn), lambda bi, i, j: (0, j), pipeline_mode=pl.Buffered(1))

    return pl.pallas_call(
        _merge_kernel,
        out_shape=jax.ShapeDtypeStruct((bsz, t_len, d), BF16),
        grid=(bsz, t_len // tm, nb),
        in_specs=[yspec(ya.shape[2]), yspec(yb.shape[2]), yspec(yc.shape[2]),
                  gspec(0), gspec(1), gspec(2),
                  wspec(wa.shape[0]), wspec(wb.shape[0]), wspec(wc.shape[0])],
        out_specs=pl.BlockSpec((1, tm, tn), lambda bi, i, j: (bi, i, j)),
        compiler_params=_cparams("arbitrary", "arbitrary", "arbitrary"),
        name="merge",
    )(ya, yb, yc, p, p, p, wa, wb, wc)


def _outproj_kernel(y_ref, w_ref, x_ref, mod_ref, g_ref, b_ref, o_ref, *, alpha):
    o = _dot(y_ref[0], w_ref[...])
    r = alpha * x_ref[0] + mod_ref[0, 2:3, :] * o
    o_ref[0] = _layer_norm_rows(r, g_ref[...], b_ref[...])


def _outproj(y, w, x, mod, ln_g, ln_b, alpha):
    bsz, t_len, d = x.shape
    tm = _pick_tile(t_len, 512)
    per_batch = mod.shape[0] > 1
    full = lambda bi, i: (0, 0)
    return pl.pallas_call(
        functools.partial(_outproj_kernel, alpha=alpha),
        out_shape=jax.ShapeDtypeStruct((bsz, t_len, d), F32),
        grid=(bsz, t_len // tm),
        in_specs=[pl.BlockSpec((1, tm, d), lambda bi, i: (bi, i, 0)),
                  pl.BlockSpec((d, d), full, pipeline_mode=pl.Buffered(1)),
                  pl.BlockSpec((1, tm, d), lambda bi, i: (bi, i, 0)),
                  pl.BlockSpec((1, N_MOD, d), (lambda bi, i: (bi, 0, 0)) if per_batch else (lambda bi, i: (0, 0, 0))),
                  pl.BlockSpec((1, d), full), pl.BlockSpec((1, d), full)],
        out_specs=pl.BlockSpec((1, tm, d), lambda bi, i: (bi, i, 0)),
        compiler_params=_cparams("arbitrary", "arbitrary"),
        name="out_proj_ln",
    )(y, w, x, mod, ln_g, ln_b)


def _ffn_kernel(x_ref, mod_ref, w1_ref, w2_ref, g_ref, b_ref, o_ref, h_scr, *, alpha):
    j = pl.program_id(2)

    @pl.when(j == 0)
    def _():
        h_scr[...] = (x_ref[0] * (1.0 + mod_ref[0, 4:5, :]) + mod_ref[0, 3:4, :]).astype(BF16)
        o_ref[...] = jnp.zeros_like(o_ref)

    a = jnp.maximum(_dot(h_scr[...], w1_ref[...]), 0.0)
    o_ref[0] += _dot((a * a).astype(BF16), w2_ref[...])

    @pl.when(j == pl.num_programs(2) - 1)
    def _():
        r = alpha * x_ref[0] + mod_ref[0, 5:6, :] * o_ref[0]
        o_ref[0] = _layer_norm_rows(r, g_ref[...], b_ref[...])


def _ffn(x, mod, w1, w2, ln_g, ln_b, alpha):
    bsz, t_len, d = x.shape
    dff = w1.shape[1]
    tm = _pick_tile(t_len, 512)
    tf = _pick_tile(dff, 1024)
    per_batch = mod.shape[0] > 1
    full = lambda bi, i, j: (0, 0)
    return pl.pallas_call(
        functools.partial(_ffn_kernel, alpha=alpha),
        out_shape=jax.ShapeDtypeStruct((bsz, t_len, d), F32),
        grid=(bsz, t_len // tm, dff // tf),
        in_specs=[pl.BlockSpec((1, tm, d), lambda bi, i, j: (bi, i, 0)),
                  pl.BlockSpec((1, N_MOD, d), (lambda bi, i, j: (bi, 0, 0)) if per_batch else (lambda bi, i, j: (0, 0, 0))),
                  pl.BlockSpec((d, tf), lambda bi, i, j: (0, j)),
                  pl.BlockSpec((tf, d), lambda bi, i, j: (j, 0)),
                  pl.BlockSpec((1, d), full), pl.BlockSpec((1, d), full)],
        out_specs=pl.BlockSpec((1, tm, d), lambda bi, i, j: (bi, i, 0)),
        scratch_shapes=[pltpu.VMEM((tm, d), BF16)],
        compiler_params=_cparams("arbitrary", "arbitrary", "arbitrary"),
        name="ffn_ln",
    )(x, mod, w1, w2, ln_g, ln_b)


def _in_proj_layout(d_model):
    sgu_w = d_model // 2
    src = {}
    off = 0
    for name, width in (("qkv", GDN_QKV_W), ("z", GDN_V_W), ("gdn_gates", 4 * GDN_HEADS),
                        ("swa_q", SWA_Q_W), ("swa_k", SWA_KV_W), ("swa_v", SWA_KV_W),
                        ("sgu_u", sgu_w), ("sgu_v", sgu_w),
                        ("gate_a", d_model), ("gate_b", d_model), ("gate_c", d_model)):
        src[name] = (off, width)
        off += width
    order = ("gate_a", "gate_b", "gate_c", "z", "swa_q", "sgu_u", "sgu_v", "qkv", "swa_k", "swa_v")
    dst = {}
    off = 0
    for name in order:
        dst[name] = off
        assert off % min(src[name][1], 2048) == 0 or name == "qkv"
        off += src[name][1]
    return src, order, dst


def kernel(x, c, ctx, c_ctx, w_ada, b_ada, w_in, b_in, gdn_conv, gdn_a_log, gdn_dt_bias, gdn_norm, swa_sinks, sgu_ln_g, sgu_ln_b, sgu_w, sgu_b, w_branch_a, w_branch_b, w_branch_c, w_out, ln_mix_g, ln_mix_b, w_ff1, w_ff2, ln_ff_g, ln_ff_b):
    bsz, t_len, d = x.shape
    depth = w_ada.shape[0]
    alpha = (2 * depth) ** 0.25
    src, order, dst = _in_proj_layout(d)

    pad_rows = (-(bsz + 1)) % SUBLANES
    cc = jnp.concatenate([c, c_ctx[None, :], jnp.zeros((pad_rows, d), F32)], axis=0)
    mod_all = _ada_mod(cc, w_ada, b_ada).reshape(depth, bsz + 1 + pad_rows, N_MOD, d)
    cos_t, sin_t = _rope_tables(t_len)
    ones_t = jnp.ones((ctx.shape[1], LANES), F32)
    zeros_t = jnp.zeros((ctx.shape[1], LANES), F32)
    masks = _gdn_masks()
    tri = _gdn_tri()
    n_gate = 4 * GDN_HEADS
    lane_pad = LANES - n_gate

    xc = ctx
    for l in range(depth):
        need_ctx = l < depth - 1
        mod_l = mod_all[l, :bsz]
        mod_c = mod_all[l, bsz:bsz + 1]

        def cols(a, name):
            s0, wd = src[name]
            return a[..., s0:s0 + wd]

        w_main = jnp.concatenate([cols(w_in[l], n) for n in order], axis=1).astype(BF16)
        b_main = jnp.concatenate([cols(b_in[l], n) for n in order], axis=0)[None, :]
        w_g = jnp.pad(cols(w_in[l], "gdn_gates"), ((0, 0), (0, lane_pad))).astype(BF16)
        b_g = jnp.pad(cols(b_in[l], "gdn_gates"), (0, lane_pad))[None, :]
        decay_pad = (GATE_RAW_DECAY, LANES - 2 * GATE_RAW_DECAY)
        alog = jnp.pad(gdn_a_log[l].reshape(-1), decay_pad)[None, :]
        dtb = jnp.pad(gdn_dt_bias[l].reshape(-1), decay_pad)[None, :]

        p_l = _inproj(x, mod_l, w_main, b_main)
        l_len = xc.shape[1]
        flat = lambda a: a.reshape(1, bsz * l_len, a.shape[-1])
        p_c = _inproj(flat(xc), mod_c, w_main, b_main).reshape(bsz, l_len, -1)
        tab_l, rows_l = _gdn_gates(x, mod_l, w_g, b_g, alog, dtb, tri)
        tab_c, rows_c = _gdn_gates(xc, mod_c, w_g, b_g, alog, dtb, tri)

        qkv_l = _gdn_conv(p_l, gdn_conv[l], dst["qkv"])
        qkv_c = _gdn_conv(p_c, gdn_conv[l], dst["qkv"])
        ya_l, ya_c = _gdn(qkv_l, qkv_c, tab_l, rows_l, tab_c, rows_c, masks, p_l, p_c, dst["z"],
                          gdn_norm[l][None, :])

        sinks = swa_sinks[l][None, :]
        q_l, kt_l, v2_l = _swa_prep(p_l, dst["swa_q"], dst["swa_k"], dst["swa_v"], cos_t, sin_t, True)
        q_c, kt_c, v2_c = _swa_prep(p_c, dst["swa_q"], dst["swa_k"], dst["swa_v"], ones_t, zeros_t, False)
        yb_l = _swa_latent(q_l, kt_l, v2_l, kt_c, v2_c, sinks)

        ws = sgu_w[l].astype(BF16)
        bs_t = sgu_b[l].T
        sg, sb = sgu_ln_g[l][None, :], sgu_ln_b[l][None, :]
        yc_l = _sgu(p_l, dst["sgu_u"], dst["sgu_v"], sg, sb, ws, bs_t)

        wa, wb, wc = w_branch_a[l].astype(BF16), w_branch_b[l].astype(BF16), w_branch_c[l].astype(BF16)
        wo = w_out[l].astype(BF16)
        w1, w2 = w_ff1[l].astype(BF16), w_ff2[l].astype(BF16)
        lmg, lmb = ln_mix_g[l][None, :], ln_mix_b[l][None, :]
        lfg, lfb = ln_ff_g[l][None, :], ln_ff_b[l][None, :]

        y_l = _merge(ya_l, yb_l, yc_l, p_l, dst["gate_a"], wa, wb, wc)
        x = _outproj(y_l, wo, x, mod_l, lmg, lmb, alpha)
        x = _ffn(x, mod_l, w1, w2, lfg, lfb, alpha)
        if need_ctx:
            yb_c = _swa_context(q_c, kt_c, v2_c, sinks)
            yc_c = _sgu(p_c, dst["sgu_u"], dst["sgu_v"], sg, sb, ws, bs_t)
            y_c = _merge(flat(ya_c), flat(yb_c), flat(yc_c), flat(p_c), dst["gate_a"], wa, wb, wc)
            xc_flat = _outproj(y_c, wo, flat(xc), mod_c, lmg, lmb, alpha)
            xc = _ffn(xc_flat, mod_c, w1, w2, lfg, lfb, alpha).reshape(bsz, l_len, d)
    return x
```

```python
import functools
import math

import jax
import jax.numpy as jnp
import numpy as np
from jax import lax
from jax.experimental import pallas as pl
from jax.experimental.pallas import tpu as pltpu

F32 = jnp.float32
BF16 = jnp.bfloat16

GRID_W = 64
GDN_HEADS = 8
GDN_DK = 128
GDN_DV = 128
GDN_CONV = 5
SWA_Q_HEADS = 16
SWA_KV_HEADS = 4
SWA_HEAD_DIM = 64
SWA_BLOCK = 128
ROPE_THETA = 10000.0
SGU_GROUPS = 8
SGU_CHUNK = 128
N_MOD = 6
LN_EPS = 1e-5
RMS_EPS = 1e-6

GDN_QK_W = GDN_HEADS * GDN_DK
GDN_V_W = GDN_HEADS * GDN_DV
GDN_QKV_W = 2 * GDN_QK_W + GDN_V_W
SWA_Q_W = SWA_Q_HEADS * SWA_HEAD_DIM
SWA_KV_W = SWA_KV_HEADS * SWA_HEAD_DIM

LANES = 128
SUBLANES = 8
VMEM_LIMIT_BYTES = 56 * 1024 * 1024

GDN_CHUNK = 128
GDN_SERIES_BLOCK = 16
assert GDN_CHUNK == GDN_DK == GDN_DV == LANES
NEG_BIG = -1e30


def _cparams(*sem):
    return pltpu.CompilerParams(dimension_semantics=sem, vmem_limit_bytes=VMEM_LIMIT_BYTES)


def _dot(a, b):
    return jnp.dot(a, b, preferred_element_type=F32)


def _dot_nt(a, b):
    return lax.dot_general(a, b, (((1,), (1,)), ((), ())), preferred_element_type=F32)


def _layer_norm_rows(r, g, b):
    mu = jnp.mean(r, axis=-1, keepdims=True)
    d = r - mu
    var = jnp.mean(d * d, axis=-1, keepdims=True)
    return d * lax.rsqrt(var + LN_EPS) * g + b


def _gelu_tanh(x):
    return x * (0.5 * (1.0 + jnp.tanh(math.sqrt(2.0 / math.pi) * (x + 0.044715 * (x * x * x)))))


def _pick_tile(n, pref):
    t = min(n, pref)
    while n % t:
        t //= 2
    return t


def _ada_kernel(c_ref, w_ref, b_ref, o_ref):
    c = c_ref[...]
    a = (c * jax.nn.sigmoid(c)).astype(BF16)
    o_ref[0] = _dot(a, w_ref[0].astype(BF16)) + b_ref[0]


def _ada_mod(cc, w_ada, b_ada):
    nl, d, n = w_ada.shape
    tn = _pick_tile(n, 1024)
    return pl.pallas_call(
        _ada_kernel,
        out_shape=jax.ShapeDtypeStruct((nl, cc.shape[0], n), F32),
        grid=(nl, n // tn),
        in_specs=[pl.BlockSpec(cc.shape, lambda l, j: (0, 0)),
                  pl.BlockSpec((1, d, tn), lambda l, j: (l, 0, j)),
                  pl.BlockSpec((1, 1, tn), lambda l, j: (l, 0, j))],
        out_specs=pl.BlockSpec((1, cc.shape[0], tn), lambda l, j: (l, 0, j)),
        compiler_params=_cparams("arbitrary", "arbitrary"),
        name="ada_mod",
    )(cc, w_ada, b_ada.reshape(nl, 1, n))


def _inproj_kernel(x_ref, mod_ref, w_ref, b_ref, o_ref, h_scr):
    @pl.when(pl.program_id(2) == 0)
    def _():
        shift = mod_ref[0, 0:1, :]
        scale = mod_ref[0, 1:2, :]
        h_scr[...] = (x_ref[0] * (1.0 + scale) + shift).astype(BF16)

    o_ref[0] = (_dot(h_scr[...], w_ref[...]) + b_ref[...]).astype(o_ref.dtype)


def _inproj(x, mod, w, b):
    bsz, t_len, d = x.shape
    n = w.shape[1]
    tm = _pick_tile(t_len, 1024)
    tn = _pick_tile(n, 1536)
    per_batch = mod.shape[0] > 1
    return pl.pallas_call(
        _inproj_kernel,
        out_shape=jax.ShapeDtypeStruct((bsz, t_len, n), BF16),
        grid=(bsz, t_len // tm, n // tn),
        in_specs=[pl.BlockSpec((1, tm, d), lambda bi, i, j: (bi, i, 0)),
                  pl.BlockSpec((1, N_MOD, d), (lambda bi, i, j: (bi, 0, 0)) if per_batch else (lambda bi, i, j: (0, 0, 0))),
                  pl.BlockSpec((d, tn), lambda bi, i, j: (0, j)),
                  pl.BlockSpec((1, tn), lambda bi, i, j: (0, j))],
        out_specs=pl.BlockSpec((1, tm, tn), lambda bi, i, j: (bi, i, j)),
        scratch_shapes=[pltpu.VMEM((tm, d), BF16)],
        compiler_params=_cparams("arbitrary", "arbitrary", "arbitrary"),
        name="in_proj",
    )(x, mod, w, b)


GATE_BETA, GATE_CUM, GATE_E, GATE_EK, GATE_ETOT = 0, 16, 32, 48, 64
DIR_STRIDE = GDN_HEADS
GATE_RAW_DECAY = 2 * GDN_HEADS


def _split3_bf16(x):
    hi = x.astype(BF16)
    r1 = x - hi.astype(F32)
    mid = r1.astype(BF16)
    lo = (r1 - mid.astype(F32)).astype(BF16)
    return hi, mid, lo


def _gdn_gate_kernel(x_ref, mod_ref, w_ref, b_ref, alog_ref, dtb_ref, tri_ref, cols_ref, rows_ref):
    c_len = GDN_CHUNK
    tm = x_ref.shape[1]
    shift = mod_ref[0, 0:1, :]
    scale = mod_ref[0, 1:2, :]
    h = (x_ref[0] * (1.0 + scale) + shift).astype(BF16)
    st = _dot(h, w_ref[...]) + b_ref[...]
    lane = lax.broadcasted_iota(jnp.int32, (c_len, LANES), 1)
    is_decay = (lane >= GATE_RAW_DECAY) & (lane < 2 * GATE_RAW_DECAY)
    tri = tri_ref[...]
    for c in range(tm // c_len):
        rs = slice(c * c_len, (c + 1) * c_len)
        stc = st[rs]
        beta = jax.nn.sigmoid(stc)
        xs = stc + dtb_ref[...]
        softplus = jnp.maximum(xs, 0.0) + jnp.log1p(jnp.exp(-jnp.abs(xs)))
        g = jnp.where(is_decay, -jnp.exp(alog_ref[...]) * softplus, 0.0)
        t3 = sum(_dot(tri, piece) for piece in _split3_bf16(g))
        tot = t3[2 * c_len:3 * c_len]
        cum = jnp.where(lane >= GATE_RAW_DECAY + DIR_STRIDE, t3[c_len:2 * c_len], t3[0:c_len])
        e = jnp.exp(cum)
        ek = jnp.exp(tot - cum)
        et = jnp.exp(tot)
        assert GATE_CUM == GATE_RAW_DECAY
        table = jnp.where(lane < GATE_CUM, beta,
                          jnp.where(lane < GATE_E, cum,
                                    jnp.where(lane < GATE_EK, pltpu.roll(e, GATE_E - GATE_RAW_DECAY, axis=1),
                                              jnp.where(lane < GATE_ETOT, pltpu.roll(ek, GATE_EK - GATE_RAW_DECAY, axis=1),
                                                        pltpu.roll(et, GATE_ETOT - GATE_RAW_DECAY, axis=1)))))
        for hh in range(GDN_HEADS):
            cols_ref[0, hh, rs, :] = pltpu.roll(table, LANES - hh, axis=1) if hh else table
        rows_ref[0, :, rs] = cum.T[GATE_RAW_DECAY:2 * GATE_RAW_DECAY]


def _gdn_gates(x, mod, w, b, alog, dtb, tri):
    bsz, t_len, d = x.shape
    tm = _pick_tile(t_len, 512)
    per_batch = mod.shape[0] > 1
    small = lambda bi, i: (0, 0)
    return pl.pallas_call(
        _gdn_gate_kernel,
        out_shape=(jax.ShapeDtypeStruct((bsz, GDN_HEADS, t_len, LANES), F32),
                   jax.ShapeDtypeStruct((bsz, 2 * GDN_HEADS, t_len), F32)),
        grid=(bsz, t_len // tm),
        in_specs=[pl.BlockSpec((1, tm, d), lambda bi, i: (bi, i, 0)),
                  pl.BlockSpec((1, N_MOD, d), (lambda bi, i: (bi, 0, 0)) if per_batch else (lambda bi, i: (0, 0, 0))),
                  pl.BlockSpec((d, LANES), small),
                  pl.BlockSpec((1, LANES), small),
                  pl.BlockSpec((1, LANES), small),
                  pl.BlockSpec((1, LANES), small),
                  pl.BlockSpec(tri.shape, small)],
        out_specs=(pl.BlockSpec((1, GDN_HEADS, tm, LANES), lambda bi, i: (bi, 0, i, 0)),
                   pl.BlockSpec((1, 2 * GDN_HEADS, tm), lambda bi, i: (bi, 0, i))),
        compiler_params=_cparams("arbitrary", "arbitrary"),
        name="gdn_gates",
    )(x, mod, w, b, alog, dtb, tri)


CONV_ROWS = 256
CONV_HALO = SUBLANES


def _gdn_conv_kernel(x_ref, w_ref, o_ref, pad_scr):
    t_len = x_ref.shape[1]
    j = pl.program_id(1)
    zeros = jnp.zeros((CONV_HALO, LANES), F32)
    pad_scr[0:CONV_HALO, :] = zeros
    pad_scr[t_len + CONV_HALO:t_len + 2 * CONV_HALO, :] = zeros
    pad_scr[CONV_HALO:t_len + CONV_HALO, :] = x_ref[0].astype(F32)
    heads_qk = 2 * GDN_HEADS
    qk_scale = jnp.where(j < GDN_HEADS, GDN_DK ** -0.5, 1.0).astype(F32)
    w = w_ref[...]
    half = GDN_CONV // 2

    def conv_silu(c):
        r0 = pl.multiple_of(c * CONV_ROWS, CONV_ROWS)
        acc = jnp.zeros((CONV_ROWS, LANES), F32)
        for k in range(GDN_CONV):
            off = CONV_HALO - half + k
            acc = acc + pad_scr[pl.ds(r0 + off, CONV_ROWS), :] * w[k:k + 1, :]
        return r0, acc * jax.nn.sigmoid(acc)

    def body_qk(c, carry):
        r0, y = conv_silu(c)
        ss = jnp.sum(y * y, axis=-1, keepdims=True)
        o_ref[0, pl.ds(r0, CONV_ROWS), :] = (y * (lax.rsqrt(ss + RMS_EPS) * qk_scale)).astype(o_ref.dtype)
        return carry

    def body_v(c, carry):
        r0, y = conv_silu(c)
        o_ref[0, pl.ds(r0, CONV_ROWS), :] = y.astype(o_ref.dtype)
        return carry

    @pl.when(j < heads_qk)
    def _():
        lax.fori_loop(0, t_len // CONV_ROWS, body_qk, 0, unroll=2 if t_len // CONV_ROWS % 2 == 0 else 1)

    @pl.when(j >= heads_qk)
    def _():
        lax.fori_loop(0, t_len // CONV_ROWS, body_v, 0)


def _gdn_conv(p, conv_w, col0):
    bsz, t_len, _ = p.shape
    nblk = GDN_QKV_W // LANES
    cb = col0 // LANES
    return pl.pallas_call(
        _gdn_conv_kernel,
        out_shape=jax.ShapeDtypeStruct((bsz, t_len, GDN_QKV_W), BF16),
        grid=(bsz, nblk),
        in_specs=[pl.BlockSpec((1, t_len, LANES), lambda bi, j: (bi, 0, cb + j)),
                  pl.BlockSpec((GDN_CONV, LANES), lambda bi, j: (0, j))],
        out_specs=pl.BlockSpec((1, t_len, LANES), lambda bi, j: (bi, 0, j)),
        scratch_shapes=[pltpu.VMEM((t_len + 2 * CONV_HALO, LANES), F32)],
        compiler_params=_cparams("arbitrary", "arbitrary"),
        name="gdn_conv",
    )(p, conv_w)


M_EYE, M_INCL, M_NEG_STRICT, M_DIAG_BLOCK, M_MERGE = 0, 1, 3, 5, 6
GDN_MERGE_LEVELS = int(math.log2(GDN_CHUNK // GDN_SERIES_BLOCK))
GDN_WIDE_STEPS = int(math.log2(GDN_SERIES_BLOCK)) - 2
GDN_TILE = 1024
GATED_NORM_ROWS = 512


def _gdn_masks():
    c = GDN_CHUNK
    i = np.arange(c)[:, None]
    j = np.arange(c)[None, :]
    rows = [i == j, j <= i, j >= i, -(j < i).astype(np.float32), -(j > i).astype(np.float32),
            (i // GDN_SERIES_BLOCK) == (j // GDN_SERIES_BLOCK)]
    s = GDN_SERIES_BLOCK
    while s < c:
        rows.append(((i // (2 * s)) == (j // (2 * s))) & ((i // s) != (j // s)))
        s *= 2
    return jnp.asarray(np.stack([np.asarray(r, np.float32) for r in rows]))


def _gdn_tri():
    c = GDN_CHUNK
    i = np.arange(c)[:, None]
    j = np.arange(c)[None, :]
    return jnp.asarray(np.concatenate([j <= i, j >= i, np.ones((c, c), bool)], axis=0).astype(np.float32), BF16)


def _in_two_halves(fn, *lists):
    half = (len(lists[0]) + 1) // 2
    out = [fn(*args) for args in zip(*(l[:half] for l in lists))]
    yield
    out += [fn(*args) for args in zip(*(l[half:] for l in lists))]
    yield
    return out


def _paired_products(lhs, rhs):
    c = GDN_CHUNK
    zero = jnp.zeros((c, c), BF16)
    n = len(lhs)
    half = -(-n // 4) * 2
    out = []
    for lo, hi in ((0, min(half, n)), (min(half, n), n)):
        for k in range(lo, hi - 1, 2):
            diag = jnp.concatenate([jnp.concatenate([rhs[k], zero], axis=1),
                                    jnp.concatenate([zero, rhs[k + 1]], axis=1)], axis=0)
            both = _dot(jnp.concatenate([lhs[k], lhs[k + 1]], axis=1), diag)
            out += [both[:, :c], both[:, c:]]
        if (hi - lo) % 2:
            out.append(_dot(lhs[hi - 1], rhs[hi - 1]))
        yield
    return out


def _gdn_intra_stages(chains_in, m_ref, refs, slot):
    c = GDN_CHUNK
    u_ref, wq_ref, akt_ref, et_ref = refs
    gqs = yield from _in_two_halves(lambda ch: _dot_nt(jnp.concatenate([ch[1], ch[0]], axis=0), ch[1]),
                                    chains_in)
    eye = m_ref[M_EYE]
    chains = []
    for (q, k, v, table, cum_row, d, idx), gq in zip(chains_in, gqs):
        def col(base, table=table, d=d):
            lane = base + d * DIR_STRIDE
            return table[:, lane:lane + 1]

        beta_c, cum_c = col(GATE_BETA), col(GATE_CUM)
        incl = m_ref[M_INCL + d]
        ex = jnp.exp((cum_c - cum_row) * incl)
        neg_m = (gq[:c] * beta_c) * (ex * m_ref[M_NEG_STRICT + d])
        chains.append(dict(q=q, k=k, v=v, table=table, idx=idx, d=d, col=col, beta_c=beta_c,
                           ex_incl=ex * incl, qk=gq[c:], neg_m=neg_m))
    ps = [ch["neg_m"] * m_ref[M_DIAG_BLOCK] for ch in chains]
    sums = [eye + p for p in ps]
    pbs = [p.astype(BF16) for p in ps]
    ps = yield from _paired_products(pbs, pbs)
    for _ in range(GDN_WIDE_STEPS):
        pbs = [p.astype(BF16) for p in ps]
        boths = yield from _in_two_halves(
            lambda pb, s: _dot(pb, jnp.concatenate([pb, s.astype(BF16)], axis=1)), pbs, sums)
        ps = [both[:, :c] for both in boths]
        sums = [s + both[:, c:] for s, both in zip(sums, boths)]
    tails = yield from _paired_products([p.astype(BF16) for p in ps], [s.astype(BF16) for s in sums])
    invs = [s + tail for s, tail in zip(sums, tails)]
    for lvl in range(GDN_MERGE_LEVELS):
        ibs = [inv.astype(BF16) for inv in invs]
        offs = [(ch["neg_m"] * m_ref[M_MERGE + lvl]).astype(BF16) for ch in chains]
        halves = yield from _paired_products(ibs, offs)
        corrs = yield from _paired_products([half.astype(BF16) for half in halves], ibs)
        invs = [inv + corr for inv, corr in zip(invs, corrs)]
    rhss = []
    for ch in chains:
        kf = ch["k"].astype(F32)
        vf = ch["v"].astype(F32)
        e_c = ch["col"](GATE_E)
        rhss.append(jnp.concatenate([(vf * ch["beta_c"]).astype(BF16), (kf * (ch["beta_c"] * e_c)).astype(BF16)], axis=1))
    uws = yield from _in_two_halves(lambda inv, rhs: _dot(inv.astype(BF16), rhs), invs, rhss)
    for ch, uw in zip(chains, uws):
        d, idx = ch["d"], ch["idx"]
        a = (ch["qk"] * ch["ex_incl"]).astype(BF16)
        qd = (ch["q"].astype(F32) * ch["col"](GATE_E)).astype(BF16)
        kdt = (ch["k"].astype(F32) * ch["col"](GATE_EK)).T.astype(BF16)
        u_ref[slot, d, idx] = uw[:, :c].astype(BF16)
        wq_ref[slot, d, idx] = jnp.concatenate([uw[:, c:].astype(BF16), qd], axis=0)
        akt_ref[slot, d, idx] = jnp.concatenate([a, kdt], axis=0)
        lane = GATE_ETOT + d * DIR_STRIDE
        et_ref[slot, d, idx] = jnp.broadcast_to(ch["table"][0:1, lane:lane + 1], (SUBLANES, LANES))


GDN_INTRA_STAGES = 2 * (3 + GDN_WIDE_STEPS + 2 * GDN_MERGE_LEVELS + 1)


def _gdn_scan_stages(refs, slot, s_scrs, o_scr, first_chunks, n):
    c = GDN_CHUNK
    u_ref, wq_ref, akt_ref, et_ref = refs
    for j in range(n):
        idxs = (j, n - 1 - j)
        ss = [s_scrs[d][...] for d in range(2)]
        ws_qs = [_dot(wq_ref[slot, d, idxs[d]], ss[d].astype(BF16)) for d in range(2)]
        yield
        v_news = [(u_ref[slot, d, idxs[d]].astype(F32) - ws_qs[d][:c]).astype(BF16) for d in range(2)]
        av_kvs = [_dot(akt_ref[slot, d, idxs[d]], v_news[d]) for d in range(2)]
        yield
        for d in range(2):
            r0 = pl.multiple_of((first_chunks[d] + idxs[d]) * c, c)
            o_scr[pl.ds(r0, c), :] += ws_qs[d][c:] + av_kvs[d][:c]
            s_scrs[d][...] = ss[d] * et_ref[slot, d, idxs[d]][0:1, :] + av_kvs[d][c:]


def _drain(stages):
    for _ in stages:
        pass


def _interleave(main, side, n_main, n_side):
    done = 0
    for k, _ in enumerate(main, 1):
        target = min(n_side, -(-k * n_side // n_main))
        while done < target:
            next(side)
            done += 1
    _drain(side)


def _gated_norm_stages(o_scr, z_ref, nw, y_ref, row0, row1):
    rows = min(GATED_NORM_ROWS, max(row1 - row0, 1))
    for r0 in range(row0, row1, rows):
        o = o_scr[r0:r0 + rows, :]
        z = z_ref[0, r0:r0 + rows, :].astype(F32)
        on = o * lax.rsqrt(jnp.mean(o * o, axis=-1, keepdims=True) + RMS_EPS) * nw
        y_ref[0, r0:r0 + rows, :] = (on * (z * jax.nn.sigmoid(z))).astype(y_ref.dtype)
        yield


def _gdn_kernel(qf_ref, kf_ref, vf_ref, tf_ref, rf_ref, qb_ref, kb_ref, vb_ref, tb_ref, rb_ref,
                qc_ref, kc_ref, vc_ref, tc_ref, rfc_ref, rbc_ref,
                m_ref, zl_ref, zc_ref, nw_ref, yl_ref, yc_ref,
                ul, wql, aktl, etl, uc, wqc, aktc, etc, ol_scr, oc_scr, sf_scr, sb_scr):
    c = GDN_CHUNK
    i = pl.program_id(2)
    n_tiles = pl.num_programs(2) - 1
    cpt = qf_ref.shape[1] // c
    n_ctx = qc_ref.shape[1] // c
    s_scrs = (sf_scr, sb_scr)
    lat_refs = (ul, wql, aktl, etl)
    ctx_refs = (uc, wqc, aktc, etc)
    cur = lax.rem(i, 2)
    prev = 1 - cur

    def chains_of(dirs_refs, n):
        chains = []
        for d, (q_ref, k_ref, v_ref, t_ref, r_ref) in dirs_refs:
            for j in range(n):
                rs = slice(j * c, (j + 1) * c)
                chains.append((q_ref[0, rs, :], k_ref[0, rs, :], v_ref[0, rs, :], t_ref[0, 0, rs, :],
                               r_ref[0, 0, j:j + 1, :], d, j))
        return chains

    def intra(slot):
        dirs_refs = ((0, (qf_ref, kf_ref, vf_ref, tf_ref, rf_ref)), (1, (qb_ref, kb_ref, vb_ref, tb_ref, rb_ref)))
        return _gdn_intra_stages(chains_of(dirs_refs, cpt), m_ref, lat_refs, slot)

    def scan(slot, step):
        first_chunks = (step * cpt, (n_tiles - 1 - step) * cpt)
        return _gdn_scan_stages(lat_refs, slot, s_scrs, ol_scr, first_chunks, cpt)

    @pl.when(i == 0)
    def _():
        sf_scr[...] = jnp.zeros_like(sf_scr)
        sb_scr[...] = jnp.zeros_like(sb_scr)
        oc_scr[...] = jnp.zeros_like(oc_scr)
        ol_scr[...] = jnp.zeros_like(ol_scr)
        dirs_refs = ((0, (qc_ref, kc_ref, vc_ref, tc_ref, rfc_ref)), (1, (qc_ref, kc_ref, vc_ref, tc_ref, rbc_ref)))

        def context():
            yield from _gdn_intra_stages(chains_of(dirs_refs, n_ctx), m_ref, ctx_refs, 0)
            yield from _gdn_scan_stages(ctx_refs, 0, s_scrs, oc_scr, (0, 0), n_ctx)

        _interleave(intra(cur), context(), GDN_INTRA_STAGES, GDN_INTRA_STAGES + 2 * n_ctx)

    @pl.when((i > 0) & (i < n_tiles))
    def _():
        _interleave(intra(cur), scan(prev, i - 1), GDN_INTRA_STAGES, 2 * cpt)

    @pl.when(i == n_tiles)
    def _():
        nw = nw_ref[...]
        t_len = ol_scr.shape[0]
        edge = min(cpt * c, t_len // 2)
        middle = _gated_norm_stages(ol_scr, zl_ref, nw, yl_ref, edge, t_len - edge)
        _interleave(scan(prev, i - 1), middle, 2 * cpt, -(-(t_len - 2 * edge) // GATED_NORM_ROWS))
        _drain(_gated_norm_stages(ol_scr, zl_ref, nw, yl_ref, 0, edge))
        _drain(_gated_norm_stages(ol_scr, zl_ref, nw, yl_ref, t_len - edge, t_len))
        _drain(_gated_norm_stages(oc_scr, zc_ref, nw, yc_ref, 0, oc_scr.shape[0]))


def _gdn(qkv_l, qkv_c, tab_l, rows_l, tab_c, rows_c, masks, p_l, p_c, z_col0, norm_w):
    bsz, t_len, _ = qkv_l.shape
    l_len = qkv_c.shape[1]
    c = GDN_CHUNK
    nh = GDN_HEADS
    tb = _pick_tile(t_len, GDN_TILE)
    n_tiles = t_len // tb
    n_lat, n_ctx = t_len // c, l_len // c
    rows_l = rows_l.reshape(bsz, 2 * nh, n_lat, c)
    rows_c = rows_c.reshape(bsz, 2 * nh, n_ctx, c)
    zb = z_col0 // LANES
    tile_of = (lambda i: jnp.minimum(i, n_tiles - 1), lambda i: jnp.maximum(n_tiles - 1 - i, 0))

    def dir_specs(d):
        tile = tile_of[d]
        return ([pl.BlockSpec((1, tb, LANES), lambda bi, h, i, off=off: (bi, tile(i), off + h))
                 for off in (0, nh, 2 * nh)]
                + [pl.BlockSpec((1, 1, tb, LANES), lambda bi, h, i: (bi, h, tile(i), 0)),
                   pl.BlockSpec((1, 1, tb // c, c), lambda bi, h, i: (bi, d * nh + h, tile(i), 0))])

    def ctx_spec(col_off):
        return pl.BlockSpec((1, l_len, LANES), lambda bi, h, i: (bi, 0, col_off + h))

    def full_spec(n_rows, col_off):
        return pl.BlockSpec((1, n_rows, LANES), lambda bi, h, i: (bi, 0, col_off + h))

    in_specs = (dir_specs(0) + dir_specs(1)
                + [ctx_spec(0), ctx_spec(nh), ctx_spec(2 * nh),
                   pl.BlockSpec((1, 1, l_len, LANES), lambda bi, h, i: (bi, h, 0, 0)),
                   pl.BlockSpec((1, 1, n_ctx, c), lambda bi, h, i: (bi, h, 0, 0)),
                   pl.BlockSpec((1, 1, n_ctx, c), lambda bi, h, i: (bi, nh + h, 0, 0)),
                   pl.BlockSpec(masks.shape, lambda bi, h, i: (0, 0, 0)),
                   full_spec(t_len, zb), full_spec(l_len, zb),
                   pl.BlockSpec((1, LANES), lambda bi, h, i: (0, 0))])

    def scratch(n_slots, n_chunk):
        lead = (n_slots, 2, n_chunk)
        return [pltpu.VMEM(lead + (c, GDN_DV), BF16), pltpu.VMEM(lead + (2 * c, GDN_DK), BF16),
                pltpu.VMEM(lead + (2 * c, c), BF16), pltpu.VMEM(lead + (SUBLANES, LANES), F32)]

    lat = (qkv_l, qkv_l, qkv_l, tab_l, rows_l)
    return pl.pallas_call(
        _gdn_kernel,
        out_shape=(jax.ShapeDtypeStruct((bsz, t_len, GDN_V_W), BF16),
                   jax.ShapeDtypeStruct((bsz, l_len, GDN_V_W), BF16)),
        grid=(bsz, nh, n_tiles + 1),
        in_specs=in_specs,
        out_specs=(full_spec(t_len, 0), full_spec(l_len, 0)),
        scratch_shapes=scratch(2, tb // c) + scratch(1, n_ctx) + [
            pltpu.VMEM((t_len, GDN_DV), F32), pltpu.VMEM((l_len, GDN_DV), F32),
            pltpu.VMEM((GDN_DK, GDN_DV), F32), pltpu.VMEM((GDN_DK, GDN_DV), F32)],
        compiler_params=_cparams("arbitrary", "arbitrary", "arbitrary"),
        name="gdn_scan",
    )(*lat, *lat, qkv_c, qkv_c, qkv_c, tab_c, rows_c, rows_c, masks, p_l, p_c, norm_w)


def _dup_groups(x):
    lane = lax.broadcasted_iota(jnp.int32, (x.shape[0], LANES), 1)
    lo = lane < LANES // 2
    outs = []
    for c in range(x.shape[1] // LANES):
        xc = x[:, c * LANES:(c + 1) * LANES]
        xs = pltpu.roll(xc, LANES // 2, axis=1)
        outs.append(jnp.where(lo, xc, xs))
        outs.append(jnp.where(lo, xs, xc))
    return outs


def _rope_kernel(q_ref, k_ref, v_ref, cos_ref, sin_ref, qo_ref, kt_ref, vo_ref, *, rope):
    q = q_ref[0].astype(F32)
    k = k_ref[0].astype(F32)
    if rope:
        cos = cos_ref[...]
        sin = sin_ref[...]
        quarter = SWA_HEAD_DIM // 2
        lane = lax.broadcasted_iota(jnp.int32, cos.shape, 1)
        first = (lane % SWA_HEAD_DIM) < quarter

        def rot(x):
            outs = []
            for c in range(x.shape[1] // LANES):
                xc = x[:, c * LANES:(c + 1) * LANES]
                partner = jnp.where(first, pltpu.roll(xc, LANES - quarter, axis=1), pltpu.roll(xc, quarter, axis=1))
                outs.append(xc * cos + partner * sin)
            return jnp.concatenate(outs, axis=1)

        q = rot(q)
        k = rot(k)
    qo_ref[0] = (q * (SWA_HEAD_DIM ** -0.5)).astype(qo_ref.dtype)
    for g, kg in enumerate(_dup_groups(k)):
        kt_ref[0, g * LANES:(g + 1) * LANES, :] = kg.T.astype(kt_ref.dtype)
    vo_ref[0] = jnp.concatenate(_dup_groups(v_ref[0].astype(F32)), axis=1).astype(vo_ref.dtype)


def _swa_prep(p, q_col0, k_col0, v_col0, cos_t, sin_t, rope):
    bsz, t_len, _ = p.shape
    tr = _pick_tile(t_len, 512)
    return pl.pallas_call(
        functools.partial(_rope_kernel, rope=rope),
        out_shape=(jax.ShapeDtypeStruct((bsz, t_len, SWA_Q_W), BF16),
                   jax.ShapeDtypeStruct((bsz, 2 * SWA_KV_W, t_len), BF16),
                   jax.ShapeDtypeStruct((bsz, t_len, 2 * SWA_KV_W), BF16)),
        grid=(bsz, t_len // tr),
        in_specs=[pl.BlockSpec((1, tr, SWA_Q_W), lambda bi, i: (bi, i, q_col0 // SWA_Q_W)),
                  pl.BlockSpec((1, tr, SWA_KV_W), lambda bi, i: (bi, i, k_col0 // SWA_KV_W)),
                  pl.BlockSpec((1, tr, SWA_KV_W), lambda bi, i: (bi, i, v_col0 // SWA_KV_W)),
                  pl.BlockSpec((tr, LANES), lambda bi, i: (i, 0)),
                  pl.BlockSpec((tr, LANES), lambda bi, i: (i, 0))],
        out_specs=(pl.BlockSpec((1, tr, SWA_Q_W), lambda bi, i: (bi, i, 0)),
                   pl.BlockSpec((1, 2 * SWA_KV_W, tr), lambda bi, i: (bi, 0, i)),
                   pl.BlockSpec((1, tr, 2 * SWA_KV_W), lambda bi, i: (bi, i, 0))),
        compiler_params=_cparams("arbitrary", "arbitrary"),
        name="swa_prep",
    )(p, p, p, cos_t, sin_t)


def _rope_tables(t_len):
    rows = t_len // GRID_W
    row = jnp.broadcast_to(jnp.arange(rows)[:, None], (rows, GRID_W)).reshape(t_len).astype(F32)
    col = jnp.broadcast_to(jnp.arange(GRID_W)[None, :], (rows, GRID_W)).reshape(t_len).astype(F32)
    n_freq = SWA_HEAD_DIM // 4
    freq = jnp.power(ROPE_THETA, -jnp.arange(n_freq, dtype=F32) / n_freq)
    ang = jnp.concatenate([row[:, None] * freq, col[:, None] * freq], axis=-1)
    cos = jnp.cos(ang)
    sin = jnp.sin(ang)
    reps = LANES // SWA_HEAD_DIM
    return (jnp.tile(jnp.concatenate([cos, cos], axis=-1), (1, reps)),
            jnp.tile(jnp.concatenate([-sin, sin], axis=-1), (1, reps)))


def _attend(q, key_blocks, sink_ref, o_ref):
    n_rows = q.shape[0]
    lane = lax.broadcasted_iota(jnp.int32, (n_rows, LANES), 1)
    lo = lane < LANES // 2
    hpg = SWA_Q_HEADS // SWA_KV_HEADS
    kt_all = jnp.concatenate([kt for kt, _, _ in key_blocks], axis=1)

    def group_scores(g):
        gs = slice(g * LANES, (g + 1) * LANES)
        q_parts, sink_parts = [], []
        for r in range(hpg):
            head = g * hpg + r
            qc = q[:, (head // 2) * LANES:(head // 2 + 1) * LANES]
            q_parts.append(jnp.where(lo if head % 2 == 0 else jnp.logical_not(lo), qc, jnp.zeros_like(qc)))
            sink_parts.append(jnp.broadcast_to(sink_ref[0:1, head:head + 1], (n_rows, 1)))
        q4 = jnp.concatenate(q_parts, axis=0)
        sink = jnp.concatenate(sink_parts, axis=0)
        s_all = _dot(q4, kt_all[gs, :])
        scores = []
        m_tile = None
        off = 0
        for _, v, mask in key_blocks:
            s = s_all[:, off:off + v.shape[0]]
            off += v.shape[0]
            if mask is not None:
                s = jnp.where(mask, s, NEG_BIG)
            scores.append(s)
            for t in range(s.shape[1] // LANES):
                st = s[:, t * LANES:(t + 1) * LANES]
                m_tile = st if m_tile is None else jnp.maximum(m_tile, st)
        return sink, scores, m_tile

    def group_finish(g, sink, scores, m_tile):
        gs = slice(g * LANES, (g + 1) * LANES)
        m = jnp.maximum(sink, jnp.max(m_tile, axis=-1, keepdims=True))
        pr = jnp.concatenate([jnp.exp(s - m).astype(BF16) for s in scores], axis=1)
        v_ones = jnp.concatenate([jnp.concatenate([v[:, gs], jnp.ones((v.shape[0], LANES), BF16)], axis=1)
                                  for _, v, _ in key_blocks], axis=0)
        acc = _dot(pr, v_ones)
        o = acc[:, :LANES] * (1.0 / (acc[:, LANES:] + jnp.exp(sink - m)))
        for pair in range(hpg // 2):
            col = (g * hpg) // 2 + pair
            even = o[(2 * pair) * n_rows:(2 * pair + 1) * n_rows]
            odd = o[(2 * pair + 1) * n_rows:(2 * pair + 2) * n_rows]
            o_ref[0, :, col * LANES:(col + 1) * LANES] = jnp.where(lo, even, odd).astype(o_ref.dtype)

    pending = group_scores(0)
    for g in range(SWA_KV_HEADS):
        following = group_scores(g + 1) if g + 1 < SWA_KV_HEADS else None
        group_finish(g, *pending)
        pending = following


def _swa_latent_kernel(q_ref, kp_ref, kc_ref, kn_ref, vp_ref, vc_ref, vn_ref, kx_ref, vx_ref, sink_ref, o_ref):
    i = pl.program_id(1)
    nblk = pl.num_programs(1)
    blk = q_ref.shape[1]
    rows4 = (SWA_Q_HEADS // SWA_KV_HEADS) * blk
    qi = lax.broadcasted_iota(jnp.int32, (rows4, blk), 0) & (blk - 1)
    kj = lax.broadcasted_iota(jnp.int32, (rows4, blk), 1)
    mask_prev = kj >= qi + jnp.where(i > 0, 0, blk)
    mask_next = kj <= qi - jnp.where(i < nblk - 1, 0, blk)
    blocks = [(kp_ref[0], vp_ref[0], mask_prev), (kc_ref[0], vc_ref[0], None),
              (kn_ref[0], vn_ref[0], mask_next), (kx_ref[0], vx_ref[0], None)]
    _attend(q_ref[0], blocks, sink_ref, o_ref)


def _swa_latent(q, kt, v2, kt_c, v2_c, sinks):
    bsz, t_len, _ = q.shape
    l_len = v2_c.shape[1]
    blk = SWA_BLOCK
    assert blk & (blk - 1) == 0
    nblk = t_len // blk
    kvw = v2.shape[2]
    prev = lambda i: jnp.maximum(i - 1, 0)
    nxt = lambda i: jnp.minimum(i + 1, nblk - 1)

    def kspec(f):
        return pl.BlockSpec((1, kvw, blk), lambda bi, i: (bi, 0, f(i)))

    def vspec(f):
        return pl.BlockSpec((1, blk, kvw), lambda bi, i: (bi, f(i), 0))

    same = lambda i: i
    return pl.pallas_call(
        _swa_latent_kernel,
        out_shape=jax.ShapeDtypeStruct((bsz, t_len, SWA_Q_W), BF16),
        grid=(bsz, nblk),
        in_specs=[pl.BlockSpec((1, blk, SWA_Q_W), lambda bi, i: (bi, i, 0)),
                  kspec(prev), kspec(same), kspec(nxt), vspec(prev), vspec(same), vspec(nxt),
                  pl.BlockSpec((1, kvw, l_len), lambda bi, i: (bi, 0, 0)),
                  pl.BlockSpec((1, l_len, kvw), lambda bi, i: (bi, 0, 0)),
                  pl.BlockSpec((1, SWA_Q_HEADS), lambda bi, i: (0, 0))],
        out_specs=pl.BlockSpec((1, blk, SWA_Q_W), lambda bi, i: (bi, i, 0)),
        compiler_params=_cparams("arbitrary", "arbitrary"),
        name="swa_latent",
    )(q, kt, kt, kt, v2, v2, v2, kt_c, v2_c, sinks)


def _swa_context_kernel(q_ref, k_ref, v_ref, sink_ref, o_ref):
    _attend(q_ref[0], [(k_ref[0], v_ref[0], None)], sink_ref, o_ref)


def _swa_context(q, kt, v2, sinks):
    bsz, l_len, _ = q.shape
    kvw = v2.shape[2]
    return pl.pallas_call(
        _swa_context_kernel,
        out_shape=jax.ShapeDtypeStruct((bsz, l_len, SWA_Q_W), BF16),
        grid=(bsz,),
        in_specs=[pl.BlockSpec((1, l_len, SWA_Q_W), lambda bi: (bi, 0, 0)),
                  pl.BlockSpec((1, kvw, l_len), lambda bi: (bi, 0, 0)),
                  pl.BlockSpec((1, l_len, kvw), lambda bi: (bi, 0, 0)),
                  pl.BlockSpec((1, SWA_Q_HEADS), lambda bi: (0, 0))],
        out_specs=pl.BlockSpec((1, l_len, SWA_Q_W), lambda bi: (bi, 0, 0)),
        compiler_params=_cparams("arbitrary"),
        name="swa_context",
    )(q, kt, v2, sinks)


def _sgu_kernel(u_ref, v_ref, g_ref, b_ref, ws_ref, bs_ref, o_ref):
    n_rows = u_ref.shape[1]
    v = _layer_norm_rows(_gelu_tanh(v_ref[0].astype(F32)), g_ref[...], b_ref[...]).astype(BF16)
    cw = v.shape[1] // SGU_GROUPS
    for c in range(n_rows // SGU_CHUNK):
        rs = slice(c * SGU_CHUNK, (c + 1) * SGU_CHUNK)
        for g in range(SGU_GROUPS):
            cs = slice(g * cw, (g + 1) * cw)
            s = _dot(ws_ref[g], v[rs, cs]) + bs_ref[:, g:g + 1]
            o_ref[0, rs, cs] = (_gelu_tanh(u_ref[0, rs, cs].astype(F32)) * s).astype(o_ref.dtype)


def _sgu(p, u_col0, v_col0, ln_g, ln_b, ws, bs_t):
    bsz, t_len, _ = p.shape
    w = ln_g.shape[1]
    tr = _pick_tile(t_len, 2 * SGU_CHUNK)
    full = lambda bi, i: (0, 0)
    return pl.pallas_call(
        _sgu_kernel,
        out_shape=jax.ShapeDtypeStruct((bsz, t_len, w), BF16),
        grid=(bsz, t_len // tr),
        in_specs=[pl.BlockSpec((1, tr, w), lambda bi, i: (bi, i, u_col0 // w)),
                  pl.BlockSpec((1, tr, w), lambda bi, i: (bi, i, v_col0 // w)),
                  pl.BlockSpec((1, w), full), pl.BlockSpec((1, w), full),
                  pl.BlockSpec(ws.shape, lambda bi, i: (0, 0, 0)),
                  pl.BlockSpec(bs_t.shape, full)],
        out_specs=pl.BlockSpec((1, tr, w), lambda bi, i: (bi, i, 0)),
        compiler_params=_cparams("arbitrary", "arbitrary"),
        name="sgu",
    )(p, p, ln_g, ln_b, ws, bs_t)


def _merge_kernel(ya_ref, yb_ref, yc_ref, ga_ref, gb_ref, gc_ref, wa_ref, wb_ref, wc_ref, o_ref):
    y = (jax.nn.sigmoid(ga_ref[0].astype(F32)) * _dot(ya_ref[0], wa_ref[...])
         + jax.nn.sigmoid(gb_ref[0].astype(F32)) * _dot(yb_ref[0], wb_ref[...])
         + jax.nn.sigmoid(gc_ref[0].astype(F32)) * _dot(yc_ref[0], wc_ref[...]))
    o_ref[0] = y.astype(o_ref.dtype)


def _merge(ya, yb, yc, p, gate_col0, wa, wb, wc):
    bsz, t_len, _ = ya.shape
    d = wa.shape[1]
    tm = _pick_tile(t_len, 512)
    tn = d
    nb = d // tn
    g0 = gate_col0 // tn

    def yspec(width):
        return pl.BlockSpec((1, tm, width), lambda bi, i, j: (bi, i, 0))

    def gspec(k):
        return pl.BlockSpec((1, tm, tn), lambda bi, i, j: (bi, i, g0 + k * nb + j))

    def wspec(width):
        return pl.BlockSpec((width, tn), lambda bi, i, j: (0, j), pipeline_mode=pl.Buffered(1))

    return pl.pallas_call(
        _merge_kernel,
        out_shape=jax.ShapeDtypeStruct((bsz, t_len, d), BF16),
        grid=(bsz, t_len // tm, nb),
        in_specs=[yspec(ya.shape[2]), yspec(yb.shape[2]), yspec(yc.shape[2]),
                  gspec(0), gspec(1), gspec(2),
                  wspec(wa.shape[0]), wspec(wb.shape[0]), wspec(wc.shape[0])],
        out_specs=pl.BlockSpec((1, tm, tn), lambda bi, i, j: (bi, i, j)),
        compiler_params=_cparams("arbitrary", "arbitrary", "arbitrary"),
        name="merge",
    )(ya, yb, yc, p, p, p, wa, wb, wc)


def _outproj_kernel(y_ref, w_ref, x_ref, mod_ref, g_ref, b_ref, o_ref, *, alpha):
    o = _dot(y_ref[0], w_ref[...])
    r = alpha * x_ref[0] + mod_ref[0, 2:3, :] * o
    o_ref[0] = _layer_norm_rows(r, g_ref[...], b_ref[...])


def _outproj(y, w, x, mod, ln_g, ln_b, alpha):
    bsz, t_len, d = x.shape
    tm = _pick_tile(t_len, 512)
    per_batch = mod.shape[0] > 1
    full = lambda bi, i: (0, 0)
    return pl.pallas_call(
        functools.partial(_outproj_kernel, alpha=alpha),
        out_shape=jax.ShapeDtypeStruct((bsz, t_len, d), F32),
        grid=(bsz, t_len // tm),
        in_specs=[pl.BlockSpec((1, tm, d), lambda bi, i: (bi, i, 0)),
                  pl.BlockSpec((d, d), full, pipeline_mode=pl.Buffered(1)),
                  pl.BlockSpec((1, tm, d), lambda bi, i: (bi, i, 0)),
                  pl.BlockSpec((1, N_MOD, d), (lambda bi, i: (bi, 0, 0)) if per_batch else (lambda bi, i: (0, 0, 0))),
                  pl.BlockSpec((1, d), full), pl.BlockSpec((1, d), full)],
        out_specs=pl.BlockSpec((1, tm, d), lambda bi, i: (bi, i, 0)),
        compiler_params=_cparams("arbitrary", "arbitrary"),
        name="out_proj_ln",
    )(y, w, x, mod, ln_g, ln_b)


def _ffn_kernel(x_ref, mod_ref, w1_ref, w2_ref, g_ref, b_ref, o_ref, h_scr, *, alpha):
    j = pl.program_id(2)

    @pl.when(j == 0)
    def _():
        h_scr[...] = (x_ref[0] * (1.0 + mod_ref[0, 4:5, :]) + mod_ref[0, 3:4, :]).astype(BF16)
        o_ref[...] = jnp.zeros_like(o_ref)

    a = jnp.maximum(_dot(h_scr[...], w1_ref[...]), 0.0)
    o_ref[0] += _dot((a * a).astype(BF16), w2_ref[...])

    @pl.when(j == pl.num_programs(2) - 1)
    def _():
        r = alpha * x_ref[0] + mod_ref[0, 5:6, :] * o_ref[0]
        o_ref[0] = _layer_norm_rows(r, g_ref[...], b_ref[...])


def _ffn(x, mod, w1, w2, ln_g, ln_b, alpha):
    bsz, t_len, d = x.shape
    dff = w1.shape[1]
    tm = _pick_tile(t_len, 512)
    tf = _pick_tile(dff, 1024)
    per_batch = mod.shape[0] > 1
    full = lambda bi, i, j: (0, 0)
    return pl.pallas_call(
        functools.partial(_ffn_kernel, alpha=alpha),
        out_shape=jax.ShapeDtypeStruct((bsz, t_len, d), F32),
        grid=(bsz, t_len // tm, dff // tf),
        in_specs=[pl.BlockSpec((1, tm, d), lambda bi, i, j: (bi, i, 0)),
                  pl.BlockSpec((1, N_MOD, d), (lambda bi, i, j: (bi, 0, 0)) if per_batch else (lambda bi, i, j: (0, 0, 0))),
                  pl.BlockSpec((d, tf), lambda bi, i, j: (0, j)),
                  pl.BlockSpec((tf, d), lambda bi, i, j: (j, 0)),
                  pl.BlockSpec((1, d), full), pl.BlockSpec((1, d), full)],
        out_specs=pl.BlockSpec((1, tm, d), lambda bi, i, j: (bi, i, 0)),
        scratch_shapes=[pltpu.VMEM((tm, d), BF16)],
        compiler_params=_cparams("arbitrary", "arbitrary", "arbitrary"),
        name="ffn_ln",
    )(x, mod, w1, w2, ln_g, ln_b)


def _in_proj_layout(d_model):
    sgu_w = d_model // 2
    src = {}
    off = 0
    for name, width in (("qkv", GDN_QKV_W), ("z", GDN_V_W), ("gdn_gates", 4 * GDN_HEADS),
                        ("swa_q", SWA_Q_W), ("swa_k", SWA_KV_W), ("swa_v", SWA_KV_W),
                        ("sgu_u", sgu_w), ("sgu_v", sgu_w),
                        ("gate_a", d_model), ("gate_b", d_model), ("gate_c", d_model)):
        src[name] = (off, width)
        off += width
    order = ("gate_a", "gate_b", "gate_c", "z", "swa_q", "sgu_u", "sgu_v", "qkv", "swa_k", "swa_v")
    dst = {}
    off = 0
    for name in order:
        dst[name] = off
        assert off % min(src[name][1], 2048) == 0 or name == "qkv"
        off += src[name][1]
    return src, order, dst


def kernel(x, c, ctx, c_ctx, w_ada, b_ada, w_in, b_in, gdn_conv, gdn_a_log, gdn_dt_bias, gdn_norm, swa_sinks, sgu_ln_g, sgu_ln_b, sgu_w, sgu_b, w_branch_a, w_branch_b, w_branch_c, w_out, ln_mix_g, ln_mix_b, w_ff1, w_ff2, ln_ff_g, ln_ff_b):
    bsz, t_len, d = x.shape
    depth = w_ada.shape[0]
    alpha = (2 * depth) ** 0.25
    src, order, dst = _in_proj_layout(d)

    pad_rows = (-(bsz + 1)) % SUBLANES
    cc = jnp.concatenate([c, c_ctx[None, :], jnp.zeros((pad_rows, d), F32)], axis=0)
    mod_all = _ada_mod(cc, w_ada, b_ada).reshape(depth, bsz + 1 + pad_rows, N_MOD, d)
    cos_t, sin_t = _rope_tables(t_len)
    ones_t = jnp.ones((ctx.shape[1], LANES), F32)
    zeros_t = jnp.zeros((ctx.shape[1], LANES), F32)
    masks = _gdn_masks()
    tri = _gdn_tri()
    n_gate = 4 * GDN_HEADS
    lane_pad = LANES - n_gate

    xc = ctx
    for l in range(depth):
        need_ctx = l < depth - 1
        mod_l = mod_all[l, :bsz]
        mod_c = mod_all[l, bsz:bsz + 1]

        def cols(a, name):
            s0, wd = src[name]
            return a[..., s0:s0 + wd]

        w_main = jnp.concatenate([cols(w_in[l], n) for n in order], axis=1).astype(BF16)
        b_main = jnp.concatenate([cols(b_in[l], n) for n in order], axis=0)[None, :]
        w_g = jnp.pad(cols(w_in[l], "gdn_gates"), ((0, 0), (0, lane_pad))).astype(BF16)
        b_g = jnp.pad(cols(b_in[l], "gdn_gates"), (0, lane_pad))[None, :]
        decay_pad = (GATE_RAW_DECAY, LANES - 2 * GATE_RAW_DECAY)
        alog = jnp.pad(gdn_a_log[l].reshape(-1), decay_pad)[None, :]
        dtb = jnp.pad(gdn_dt_bias[l].reshape(-1), decay_pad)[None, :]

        p_l = _inproj(x, mod_l, w_main, b_main)
        l_len = xc.shape[1]
        flat = lambda a: a.reshape(1, bsz * l_len, a.shape[-1])
        p_c = _inproj(flat(xc), mod_c, w_main, b_main).reshape(bsz, l_len, -1)
        tab_l, rows_l = _gdn_gates(x, mod_l, w_g, b_g, alog, dtb, tri)
        tab_c, rows_c = _gdn_gates(xc, mod_c, w_g, b_g, alog, dtb, tri)

        qkv_l = _gdn_conv(p_l, gdn_conv[l], dst["qkv"])
        qkv_c = _gdn_conv(p_c, gdn_conv[l], dst["qkv"])
        ya_l, ya_c = _gdn(qkv_l, qkv_c, tab_l, rows_l, tab_c, rows_c, masks, p_l, p_c, dst["z"],
                          gdn_norm[l][None, :])

        sinks = swa_sinks[l][None, :]
        q_l, kt_l, v2_l = _swa_prep(p_l, dst["swa_q"], dst["swa_k"], dst["swa_v"], cos_t, sin_t, True)
        q_c, kt_c, v2_c = _swa_prep(p_c, dst["swa_q"], dst["swa_k"], dst["swa_v"], ones_t, zeros_t, False)
        yb_l = _swa_latent(q_l, kt_l, v2_l, kt_c, v2_c, sinks)

        ws = sgu_w[l].astype(BF16)
        bs_t = sgu_b[l].T
        sg, sb = sgu_ln_g[l][None, :], sgu_ln_b[l][None, :]
        yc_l = _sgu(p_l, dst["sgu_u"], dst["sgu_v"], sg, sb, ws, bs_t)

        wa, wb, wc = w_branch_a[l].astype(BF16), w_branch_b[l].astype(BF16), w_branch_c[l].astype(BF16)
        wo = w_out[l].astype(BF16)
        w1, w2 = w_ff1[l].astype(BF16), w_ff2[l].astype(BF16)
        lmg, lmb = ln_mix_g[l][None, :], ln_mix_b[l][None, :]
        lfg, lfb = ln_ff_g[l][None, :], ln_ff_b[l][None, :]

        y_l = _merge(ya_l, yb_l, yc_l, p_l, dst["gate_a"], wa, wb, wc)
        x = _outproj(y_l, wo, x, mod_l, lmg, lmb, alpha)
        x = _ffn(x, mod_l, w1, w2, lfg, lfb, alpha)
        if need_ctx:
            yb_c = _swa_context(q_c, kt_c, v2_c, sinks)
            yc_c = _sgu(p_c, dst["sgu_u"], dst["sgu_v"], sg, sb, ws, bs_t)
            y_c = _merge(flat(ya_c), flat(yb_c), flat(yc_c), flat(p_c), dst["gate_a"], wa, wb, wc)
            xc_flat = _outproj(y_c, wo, flat(xc), mod_c, lmg, lmb, alpha)
            xc = _ffn(xc_flat, mod_c, w1, w2, lfg, lfb, alpha).reshape(bsz, l_len, d)
    return x
```

```python
import functools
import math

import jax
import jax.numpy as jnp
import numpy as np
from jax import lax
from jax.experimental import pallas as pl
from jax.experimental.pallas import tpu as pltpu

F32 = jnp.float32
BF16 = jnp.bfloat16

GRID_W = 64
GDN_HEADS = 8
GDN_DK = 128
GDN_DV = 128
GDN_CONV = 5
SWA_Q_HEADS = 16
SWA_KV_HEADS = 4
SWA_HEAD_DIM = 64
SWA_BLOCK = 128
ROPE_THETA = 10000.0
SGU_GROUPS = 8
SGU_CHUNK = 128
N_MOD = 6
LN_EPS = 1e-5
RMS_EPS = 1e-6

GDN_QK_W = GDN_HEADS * GDN_DK
GDN_V_W = GDN_HEADS * GDN_DV
GDN_QKV_W = 2 * GDN_QK_W + GDN_V_W
SWA_Q_W = SWA_Q_HEADS * SWA_HEAD_DIM
SWA_KV_W = SWA_KV_HEADS * SWA_HEAD_DIM

LANES = 128
SUBLANES = 8
VMEM_LIMIT_BYTES = 56 * 1024 * 1024

GDN_CHUNK = 128
GDN_SERIES_BLOCK = 16
assert GDN_CHUNK == GDN_DK == GDN_DV == LANES
NEG_BIG = -1e30


def _cparams(*sem):
    return pltpu.CompilerParams(dimension_semantics=sem, vmem_limit_bytes=VMEM_LIMIT_BYTES)


def _dot(a, b):
    return jnp.dot(a, b, preferred_element_type=F32)


def _dot_nt(a, b):
    return lax.dot_general(a, b, (((1,), (1,)), ((), ())), preferred_element_type=F32)


def _layer_norm_rows(r, g, b):
    mu = jnp.mean(r, axis=-1, keepdims=True)
    d = r - mu
    var = jnp.mean(d * d, axis=-1, keepdims=True)
    return d * lax.rsqrt(var + LN_EPS) * g + b


def _gelu_tanh(x):
    return x * (0.5 * (1.0 + jnp.tanh(math.sqrt(2.0 / math.pi) * (x + 0.044715 * (x * x * x)))))


def _pick_tile(n, pref):
    t = min(n, pref)
    while n % t:
        t //= 2
    return t


def _ada_kernel(c_ref, w_ref, b_ref, o_ref):
    c = c_ref[...]
    a = (c * jax.nn.sigmoid(c)).astype(BF16)
    o_ref[0] = _dot(a, w_ref[0].astype(BF16)) + b_ref[0]


def _ada_mod(cc, w_ada, b_ada):
    nl, d, n = w_ada.shape
    tn = _pick_tile(n, 1024)
    return pl.pallas_call(
        _ada_kernel,
        out_shape=jax.ShapeDtypeStruct((nl, cc.shape[0], n), F32),
        grid=(nl, n // tn),
        in_specs=[pl.BlockSpec(cc.shape, lambda l, j: (0, 0)),
                  pl.BlockSpec((1, d, tn), lambda l, j: (l, 0, j)),
                  pl.BlockSpec((1, 1, tn), lambda l, j: (l, 0, j))],
        out_specs=pl.BlockSpec((1, cc.shape[0], tn), lambda l, j: (l, 0, j)),
        compiler_params=_cparams("arbitrary", "arbitrary"),
        name="ada_mod",
    )(cc, w_ada, b_ada.reshape(nl, 1, n))


def _inproj_kernel(x_ref, mod_ref, w_ref, b_ref, wg_ref, bg_ref, o_ref, g_ref, h_scr):
    @pl.when(pl.program_id(2) == 0)
    def _():
        shift = mod_ref[0, 0:1, :]
        scale = mod_ref[0, 1:2, :]
        h_scr[...] = (x_ref[0] * (1.0 + scale) + shift).astype(BF16)
        g_ref[0] = _dot(h_scr[...], wg_ref[...]) + bg_ref[...]

    o_ref[0] = (_dot(h_scr[...], w_ref[...]) + b_ref[...]).astype(o_ref.dtype)


def _inproj(x, mod, w, b, wg, bg):
    bsz, t_len, d = x.shape
    n = w.shape[1]
    tm = _pick_tile(t_len, 1024)
    tn = _pick_tile(n, 1536)
    per_batch = mod.shape[0] > 1
    return pl.pallas_call(
        _inproj_kernel,
        out_shape=(jax.ShapeDtypeStruct((bsz, t_len, n), BF16),
                   jax.ShapeDtypeStruct((bsz, t_len, LANES), F32)),
        grid=(bsz, t_len // tm, n // tn),
        in_specs=[pl.BlockSpec((1, tm, d), lambda bi, i, j: (bi, i, 0)),
                  pl.BlockSpec((1, N_MOD, d), (lambda bi, i, j: (bi, 0, 0)) if per_batch else (lambda bi, i, j: (0, 0, 0))),
                  pl.BlockSpec((d, tn), lambda bi, i, j: (0, j)),
                  pl.BlockSpec((1, tn), lambda bi, i, j: (0, j)),
                  pl.BlockSpec((d, LANES), lambda bi, i, j: (0, 0)),
                  pl.BlockSpec((1, LANES), lambda bi, i, j: (0, 0))],
        out_specs=(pl.BlockSpec((1, tm, tn), lambda bi, i, j: (bi, i, j)),
                   pl.BlockSpec((1, tm, LANES), lambda bi, i, j: (bi, i, 0))),
        scratch_shapes=[pltpu.VMEM((tm, d), BF16)],
        compiler_params=_cparams("arbitrary", "arbitrary", "arbitrary"),
        name="in_proj",
    )(x, mod, w, b, wg, bg)


GATE_BETA, GATE_CUM, GATE_E, GATE_EK, GATE_ETOT = 0, 16, 32, 48, 64
DIR_STRIDE = GDN_HEADS
GATE_RAW_DECAY = 2 * GDN_HEADS


def _split3_bf16(x):
    hi = x.astype(BF16)
    r1 = x - hi.astype(F32)
    mid = r1.astype(BF16)
    lo = (r1 - mid.astype(F32)).astype(BF16)
    return hi, mid, lo


def _gdn_gate_kernel(st_ref, alog_ref, dtb_ref, tri_ref, cols_ref, rows_ref):
    c_len = GDN_CHUNK
    tm = st_ref.shape[1]
    st = st_ref[0]
    lane = lax.broadcasted_iota(jnp.int32, (c_len, LANES), 1)
    is_decay = (lane >= GATE_RAW_DECAY) & (lane < 2 * GATE_RAW_DECAY)
    tri = tri_ref[...]
    for c in range(tm // c_len):
        rs = slice(c * c_len, (c + 1) * c_len)
        stc = st[rs]
        beta = jax.nn.sigmoid(stc)
        xs = stc + dtb_ref[...]
        softplus = jnp.maximum(xs, 0.0) + jnp.log1p(jnp.exp(-jnp.abs(xs)))
        g = jnp.where(is_decay, -jnp.exp(alog_ref[...]) * softplus, 0.0)
        t3 = sum(_dot(tri, piece) for piece in _split3_bf16(g))
        tot = t3[2 * c_len:3 * c_len]
        cum = jnp.where(lane >= GATE_RAW_DECAY + DIR_STRIDE, t3[c_len:2 * c_len], t3[0:c_len])
        e = jnp.exp(cum)
        ek = jnp.exp(tot - cum)
        et = jnp.exp(tot)
        assert GATE_CUM == GATE_RAW_DECAY
        table = jnp.where(lane < GATE_CUM, beta,
                          jnp.where(lane < GATE_E, cum,
                                    jnp.where(lane < GATE_EK, pltpu.roll(e, GATE_E - GATE_RAW_DECAY, axis=1),
                                              jnp.where(lane < GATE_ETOT, pltpu.roll(ek, GATE_EK - GATE_RAW_DECAY, axis=1),
                                                        pltpu.roll(et, GATE_ETOT - GATE_RAW_DECAY, axis=1)))))
        for hh in range(GDN_HEADS):
            cols_ref[0, hh, rs, :] = pltpu.roll(table, LANES - hh, axis=1) if hh else table
        rows_ref[0, :, rs] = cum.T[GATE_RAW_DECAY:2 * GATE_RAW_DECAY]


def _gdn_gates(st, alog, dtb, tri):
    bsz, t_len, _ = st.shape
    tm = _pick_tile(t_len, 512)
    small = lambda bi, i: (0, 0)
    return pl.pallas_call(
        _gdn_gate_kernel,
        out_shape=(jax.ShapeDtypeStruct((bsz, GDN_HEADS, t_len, LANES), F32),
                   jax.ShapeDtypeStruct((bsz, 2 * GDN_HEADS, t_len), F32)),
        grid=(bsz, t_len // tm),
        in_specs=[pl.BlockSpec((1, tm, LANES), lambda bi, i: (bi, i, 0)),
                  pl.BlockSpec((1, LANES), small),
                  pl.BlockSpec((1, LANES), small),
                  pl.BlockSpec(tri.shape, small)],
        out_specs=(pl.BlockSpec((1, GDN_HEADS, tm, LANES), lambda bi, i: (bi, 0, i, 0)),
                   pl.BlockSpec((1, 2 * GDN_HEADS, tm), lambda bi, i: (bi, 0, i))),
        compiler_params=_cparams("arbitrary", "arbitrary"),
        name="gdn_gates",
    )(st, alog, dtb, tri)


CONV_ROWS = 256
CONV_HALO = SUBLANES


def _gdn_conv_kernel(x_ref, w_ref, o_ref, pad_scr):
    t_len = x_ref.shape[1]
    j = pl.program_id(1)
    zeros = jnp.zeros((CONV_HALO, LANES), F32)
    pad_scr[0:CONV_HALO, :] = zeros
    pad_scr[t_len + CONV_HALO:t_len + 2 * CONV_HALO, :] = zeros
    pad_scr[CONV_HALO:t_len + CONV_HALO, :] = x_ref[0].astype(F32)
    heads_qk = 2 * GDN_HEADS
    qk_scale = jnp.where(j < GDN_HEADS, GDN_DK ** -0.5, 1.0).astype(F32)
    w = w_ref[...]
    half = GDN_CONV // 2

    def conv_silu(c):
        r0 = pl.multiple_of(c * CONV_ROWS, CONV_ROWS)
        acc = jnp.zeros((CONV_ROWS, LANES), F32)
        for k in range(GDN_CONV):
            off = CONV_HALO - half + k
            acc = acc + pad_scr[pl.ds(r0 + off, CONV_ROWS), :] * w[k:k + 1, :]
        return r0, acc * jax.nn.sigmoid(acc)

    def body_qk(c, carry):
        r0, y = conv_silu(c)
        ss = jnp.sum(y * y, axis=-1, keepdims=True)
        o_ref[0, pl.ds(r0, CONV_ROWS), :] = (y * (lax.rsqrt(ss + RMS_EPS) * qk_scale)).astype(o_ref.dtype)
        return carry

    def body_v(c, carry):
        r0, y = conv_silu(c)
        o_ref[0, pl.ds(r0, CONV_ROWS), :] = y.astype(o_ref.dtype)
        return carry

    @pl.when(j < heads_qk)
    def _():
        lax.fori_loop(0, t_len // CONV_ROWS, body_qk, 0, unroll=2 if t_len // CONV_ROWS % 2 == 0 else 1)

    @pl.when(j >= heads_qk)
    def _():
        lax.fori_loop(0, t_len // CONV_ROWS, body_v, 0)


def _gdn_conv(p, conv_w, col0):
    bsz, t_len, _ = p.shape
    nblk = GDN_QKV_W // LANES
    cb = col0 // LANES
    return pl.pallas_call(
        _gdn_conv_kernel,
        out_shape=jax.ShapeDtypeStruct((bsz, t_len, GDN_QKV_W), BF16),
        grid=(bsz, nblk),
        in_specs=[pl.BlockSpec((1, t_len, LANES), lambda bi, j: (bi, 0, cb + j)),
                  pl.BlockSpec((GDN_CONV, LANES), lambda bi, j: (0, j))],
        out_specs=pl.BlockSpec((1, t_len, LANES), lambda bi, j: (bi, 0, j)),
        scratch_shapes=[pltpu.VMEM((t_len + 2 * CONV_HALO, LANES), F32)],
        compiler_params=_cparams("arbitrary", "arbitrary"),
        name="gdn_conv",
    )(p, conv_w)


M_EYE, M_INCL, M_NEG_STRICT, M_DIAG_BLOCK, M_MERGE = 0, 1, 3, 5, 6
GDN_MERGE_LEVELS = int(math.log2(GDN_CHUNK // GDN_SERIES_BLOCK))
GDN_WIDE_STEPS = int(math.log2(GDN_SERIES_BLOCK)) - 2
GDN_TILE = 1024
GATED_NORM_ROWS = 512


def _gdn_masks():
    c = GDN_CHUNK
    i = np.arange(c)[:, None]
    j = np.arange(c)[None, :]
    rows = [i == j, j <= i, j >= i, -(j < i).astype(np.float32), -(j > i).astype(np.float32),
            (i // GDN_SERIES_BLOCK) == (j // GDN_SERIES_BLOCK)]
    s = GDN_SERIES_BLOCK
    while s < c:
        rows.append(((i // (2 * s)) == (j // (2 * s))) & ((i // s) != (j // s)))
        s *= 2
    return jnp.asarray(np.stack([np.asarray(r, np.float32) for r in rows]))


def _gdn_tri():
    c = GDN_CHUNK
    i = np.arange(c)[:, None]
    j = np.arange(c)[None, :]
    return jnp.asarray(np.concatenate([j <= i, j >= i, np.ones((c, c), bool)], axis=0).astype(np.float32), BF16)


def _in_two_halves(fn, *lists):
    half = (len(lists[0]) + 1) // 2
    out = [fn(*args) for args in zip(*(l[:half] for l in lists))]
    yield
    out += [fn(*args) for args in zip(*(l[half:] for l in lists))]
    yield
    return out


def _paired_products(lhs, rhs):
    c = GDN_CHUNK
    zero = jnp.zeros((c, c), BF16)
    n = len(lhs)
    half = -(-n // 4) * 2
    out = []
    for lo, hi in ((0, min(half, n)), (min(half, n), n)):
        for k in range(lo, hi - 1, 2):
            diag = jnp.concatenate([jnp.concatenate([rhs[k], zero], axis=1),
                                    jnp.concatenate([zero, rhs[k + 1]], axis=1)], axis=0)
            both = _dot(jnp.concatenate([lhs[k], lhs[k + 1]], axis=1), diag)
            out += [both[:, :c], both[:, c:]]
        if (hi - lo) % 2:
            out.append(_dot(lhs[hi - 1], rhs[hi - 1]))
        yield
    return out


def _gdn_intra_stages(chains_in, m_ref, refs, slot):
    c = GDN_CHUNK
    u_ref, wq_ref, akt_ref, et_ref = refs
    gqs = yield from _in_two_halves(lambda ch: _dot_nt(jnp.concatenate([ch[1], ch[0]], axis=0), ch[1]),
                                    chains_in)
    eye = m_ref[M_EYE]
    chains = []
    for (q, k, v, table, cum_row, d, idx), gq in zip(chains_in, gqs):
        def col(base, table=table, d=d):
            lane = base + d * DIR_STRIDE
            return table[:, lane:lane + 1]

        beta_c, cum_c = col(GATE_BETA), col(GATE_CUM)
        incl = m_ref[M_INCL + d]
        ex = jnp.exp((cum_c - cum_row) * incl)
        neg_m = (gq[:c] * beta_c) * (ex * m_ref[M_NEG_STRICT + d])
        chains.append(dict(q=q, k=k, v=v, table=table, idx=idx, d=d, col=col, beta_c=beta_c,
                           ex_incl=ex * incl, qk=gq[c:], neg_m=neg_m))
    ps = [ch["neg_m"] * m_ref[M_DIAG_BLOCK] for ch in chains]
    sums = [eye + p for p in ps]
    pbs = [p.astype(BF16) for p in ps]
    ps = yield from _paired_products(pbs, pbs)
    for _ in range(GDN_WIDE_STEPS):
        pbs = [p.astype(BF16) for p in ps]
        boths = yield from _in_two_halves(
            lambda pb, s: _dot(pb, jnp.concatenate([pb, s.astype(BF16)], axis=1)), pbs, sums)
        ps = [both[:, :c] for both in boths]
        sums = [s + both[:, c:] for s, both in zip(sums, boths)]
    tails = yield from _paired_products([p.astype(BF16) for p in ps], [s.astype(BF16) for s in sums])
    invs = [s + tail for s, tail in zip(sums, tails)]
    for lvl in range(GDN_MERGE_LEVELS):
        ibs = [inv.astype(BF16) for inv in invs]
        offs = [(ch["neg_m"] * m_ref[M_MERGE + lvl]).astype(BF16) for ch in chains]
        halves = yield from _paired_products(ibs, offs)
        corrs = yield from _paired_products([half.astype(BF16) for half in halves], ibs)
        invs = [inv + corr for inv, corr in zip(invs, corrs)]
    rhss = []
    for ch in chains:
        kf = ch["k"].astype(F32)
        vf = ch["v"].astype(F32)
        e_c = ch["col"](GATE_E)
        rhss.append(jnp.concatenate([(vf * ch["beta_c"]).astype(BF16), (kf * (ch["beta_c"] * e_c)).astype(BF16)], axis=1))
    uws = yield from _in_two_halves(lambda inv, rhs: _dot(inv.astype(BF16), rhs), invs, rhss)
    for ch, uw in zip(chains, uws):
        d, idx = ch["d"], ch["idx"]
        a = (ch["qk"] * ch["ex_incl"]).astype(BF16)
        qd = (ch["q"].astype(F32) * ch["col"](GATE_E)).astype(BF16)
        kdt = (ch["k"].astype(F32) * ch["col"](GATE_EK)).T.astype(BF16)
        u_ref[slot, d, idx] = uw[:, :c].astype(BF16)
        wq_ref[slot, d, idx] = jnp.concatenate([uw[:, c:].astype(BF16), qd], axis=0)
        akt_ref[slot, d, idx] = jnp.concatenate([a, kdt], axis=0)
        lane = GATE_ETOT + d * DIR_STRIDE
        et_ref[slot, d, idx] = jnp.broadcast_to(ch["table"][0:1, lane:lane + 1], (SUBLANES, LANES))


GDN_INTRA_STAGES = 2 * (3 + GDN_WIDE_STEPS + 2 * GDN_MERGE_LEVELS + 1)


def _gdn_scan_stages(refs, slot, s_scrs, o_scr, first_chunks, n):
    c = GDN_CHUNK
    u_ref, wq_ref, akt_ref, et_ref = refs
    for j in range(n):
        idxs = (j, n - 1 - j)
        ss = [s_scrs[d][...] for d in range(2)]
        ws_qs = [_dot(wq_ref[slot, d, idxs[d]], ss[d].astype(BF16)) for d in range(2)]
        yield
        v_news = [(u_ref[slot, d, idxs[d]].astype(F32) - ws_qs[d][:c]).astype(BF16) for d in range(2)]
        av_kvs = [_dot(akt_ref[slot, d, idxs[d]], v_news[d]) for d in range(2)]
        yield
        for d in range(2):
            r0 = pl.multiple_of((first_chunks[d] + idxs[d]) * c, c)
            o_scr[pl.ds(r0, c), :] += ws_qs[d][c:] + av_kvs[d][:c]
            s_scrs[d][...] = ss[d] * et_ref[slot, d, idxs[d]][0:1, :] + av_kvs[d][c:]


def _drain(stages):
    for _ in stages:
        pass


def _interleave(main, side, n_main, n_side):
    done = 0
    for k, _ in enumerate(main, 1):
        target = min(n_side, -(-k * n_side // n_main))
        while done < target:
            next(side)
            done += 1
    _drain(side)


def _gated_norm_stages(o_scr, z_ref, nw, y_ref, row0, row1):
    rows = min(GATED_NORM_ROWS, max(row1 - row0, 1))
    for r0 in range(row0, row1, rows):
        o = o_scr[r0:r0 + rows, :]
        z = z_ref[0, r0:r0 + rows, :].astype(F32)
        on = o * lax.rsqrt(jnp.mean(o * o, axis=-1, keepdims=True) + RMS_EPS) * nw
        y_ref[0, r0:r0 + rows, :] = (on * (z * jax.nn.sigmoid(z))).astype(y_ref.dtype)
        yield


def _gdn_kernel(qf_ref, kf_ref, vf_ref, tf_ref, rf_ref, qb_ref, kb_ref, vb_ref, tb_ref, rb_ref,
                qc_ref, kc_ref, vc_ref, tc_ref, rfc_ref, rbc_ref,
                m_ref, zl_ref, zc_ref, nw_ref, yl_ref, yc_ref,
                ul, wql, aktl, etl, uc, wqc, aktc, etc, ol_scr, oc_scr, sf_scr, sb_scr):
    c = GDN_CHUNK
    i = pl.program_id(2)
    n_tiles = pl.num_programs(2) - 1
    cpt = qf_ref.shape[1] // c
    n_ctx = qc_ref.shape[1] // c
    s_scrs = (sf_scr, sb_scr)
    lat_refs = (ul, wql, aktl, etl)
    ctx_refs = (uc, wqc, aktc, etc)
    cur = lax.rem(i, 2)
    prev = 1 - cur

    def chains_of(dirs_refs, n):
        chains = []
        for d, (q_ref, k_ref, v_ref, t_ref, r_ref) in dirs_refs:
            for j in range(n):
                rs = slice(j * c, (j + 1) * c)
                chains.append((q_ref[0, rs, :], k_ref[0, rs, :], v_ref[0, rs, :], t_ref[0, 0, rs, :],
                               r_ref[0, 0, j:j + 1, :], d, j))
        return chains

    def intra(slot):
        dirs_refs = ((0, (qf_ref, kf_ref, vf_ref, tf_ref, rf_ref)), (1, (qb_ref, kb_ref, vb_ref, tb_ref, rb_ref)))
        return _gdn_intra_stages(chains_of(dirs_refs, cpt), m_ref, lat_refs, slot)

    def scan(slot, step):
        first_chunks = (step * cpt, (n_tiles - 1 - step) * cpt)
        return _gdn_scan_stages(lat_refs, slot, s_scrs, ol_scr, first_chunks, cpt)

    @pl.when(i == 0)
    def _():
        sf_scr[...] = jnp.zeros_like(sf_scr)
        sb_scr[...] = jnp.zeros_like(sb_scr)
        oc_scr[...] = jnp.zeros_like(oc_scr)
        ol_scr[...] = jnp.zeros_like(ol_scr)
        dirs_refs = ((0, (qc_ref, kc_ref, vc_ref, tc_ref, rfc_ref)), (1, (qc_ref, kc_ref, vc_ref, tc_ref, rbc_ref)))

        def context():
            yield from _gdn_intra_stages(chains_of(dirs_refs, n_ctx), m_ref, ctx_refs, 0)
            yield from _gdn_scan_stages(ctx_refs, 0, s_scrs, oc_scr, (0, 0), n_ctx)

        _interleave(intra(cur), context(), GDN_INTRA_STAGES, GDN_INTRA_STAGES + 2 * n_ctx)

    @pl.when((i > 0) & (i < n_tiles))
    def _():
        _interleave(intra(cur), scan(prev, i - 1), GDN_INTRA_STAGES, 2 * cpt)

    @pl.when(i == n_tiles)
    def _():
        nw = nw_ref[...]
        t_len = ol_scr.shape[0]
        edge = min(cpt * c, t_len // 2)
        middle = _gated_norm_stages(ol_scr, zl_ref, nw, yl_ref, edge, t_len - edge)
        _interleave(scan(prev, i - 1), middle, 2 * cpt, -(-(t_len - 2 * edge) // GATED_NORM_ROWS))
        _drain(_gated_norm_stages(ol_scr, zl_ref, nw, yl_ref, 0, edge))
        _drain(_gated_norm_stages(ol_scr, zl_ref, nw, yl_ref, t_len - edge, t_len))
        _drain(_gated_norm_stages(oc_scr, zc_ref, nw, yc_ref, 0, oc_scr.shape[0]))


def _gdn(qkv_l, qkv_c, tab_l, rows_l, tab_c, rows_c, masks, p_l, p_c, z_col0, norm_w):
    bsz, t_len, _ = qkv_l.shape
    l_len = qkv_c.shape[1]
    c = GDN_CHUNK
    nh = GDN_HEADS
    tb = _pick_tile(t_len, GDN_TILE)
    n_tiles = t_len // tb
    n_lat, n_ctx = t_len // c, l_len // c
    rows_l = rows_l.reshape(bsz, 2 * nh, n_lat, c)
    rows_c = rows_c.reshape(bsz, 2 * nh, n_ctx, c)
    zb = z_col0 // LANES
    tile_of = (lambda i: jnp.minimum(i, n_tiles - 1), lambda i: jnp.maximum(n_tiles - 1 - i, 0))

    def dir_specs(d):
        tile = tile_of[d]
        return ([pl.BlockSpec((1, tb, LANES), lambda bi, h, i, off=off: (bi, tile(i), off + h))
                 for off in (0, nh, 2 * nh)]
                + [pl.BlockSpec((1, 1, tb, LANES), lambda bi, h, i: (bi, h, tile(i), 0)),
                   pl.BlockSpec((1, 1, tb // c, c), lambda bi, h, i: (bi, d * nh + h, tile(i), 0))])

    def ctx_spec(col_off):
        return pl.BlockSpec((1, l_len, LANES), lambda bi, h, i: (bi, 0, col_off + h))

    def full_spec(n_rows, col_off):
        return pl.BlockSpec((1, n_rows, LANES), lambda bi, h, i: (bi, 0, col_off + h))

    in_specs = (dir_specs(0) + dir_specs(1)
                + [ctx_spec(0), ctx_spec(nh), ctx_spec(2 * nh),
                   pl.BlockSpec((1, 1, l_len, LANES), lambda bi, h, i: (bi, h, 0, 0)),
                   pl.BlockSpec((1, 1, n_ctx, c), lambda bi, h, i: (bi, h, 0, 0)),
                   pl.BlockSpec((1, 1, n_ctx, c), lambda bi, h, i: (bi, nh + h, 0, 0)),
                   pl.BlockSpec(masks.shape, lambda bi, h, i: (0, 0, 0)),
                   full_spec(t_len, zb), full_spec(l_len, zb),
                   pl.BlockSpec((1, LANES), lambda bi, h, i: (0, 0))])

    def scratch(n_slots, n_chunk):
        lead = (n_slots, 2, n_chunk)
        return [pltpu.VMEM(lead + (c, GDN_DV), BF16), pltpu.VMEM(lead + (2 * c, GDN_DK), BF16),
                pltpu.VMEM(lead + (2 * c, c), BF16), pltpu.VMEM(lead + (SUBLANES, LANES), F32)]

    lat = (qkv_l, qkv_l, qkv_l, tab_l, rows_l)
    return pl.pallas_call(
        _gdn_kernel,
        out_shape=(jax.ShapeDtypeStruct((bsz, t_len, GDN_V_W), BF16),
                   jax.ShapeDtypeStruct((bsz, l_len, GDN_V_W), BF16)),
        grid=(bsz, nh, n_tiles + 1),
        in_specs=in_specs,
        out_specs=(full_spec(t_len, 0), full_spec(l_len, 0)),
        scratch_shapes=scratch(2, tb // c) + scratch(1, n_ctx) + [
            pltpu.VMEM((t_len, GDN_DV), F32), pltpu.VMEM((l_len, GDN_DV), F32),
            pltpu.VMEM((GDN_DK, GDN_DV), F32), pltpu.VMEM((GDN_DK, GDN_DV), F32)],
        compiler_params=_cparams("arbitrary", "arbitrary", "arbitrary"),
        name="gdn_scan",
    )(*lat, *lat, qkv_c, qkv_c, qkv_c, tab_c, rows_c, rows_c, masks, p_l, p_c, norm_w)


def _dup_groups(x):
    lane = lax.broadcasted_iota(jnp.int32, (x.shape[0], LANES), 1)
    lo = lane < LANES // 2
    outs = []
    for c in range(x.shape[1] // LANES):
        xc = x[:, c * LANES:(c + 1) * LANES]
        xs = pltpu.roll(xc, LANES // 2, axis=1)
        outs.append(jnp.where(lo, xc, xs))
        outs.append(jnp.where(lo, xs, xc))
    return outs


def _rope_kernel(q_ref, k_ref, v_ref, cos_ref, sin_ref, qo_ref, kt_ref, vo_ref, *, rope):
    q = q_ref[0].astype(F32)
    k = k_ref[0].astype(F32)
    if rope:
        cos = cos_ref[...]
        sin = sin_ref[...]
        quarter = SWA_HEAD_DIM // 2
        lane = lax.broadcasted_iota(jnp.int32, cos.shape, 1)
        first = (lane % SWA_HEAD_DIM) < quarter

        def rot(x):
            outs = []
            for c in range(x.shape[1] // LANES):
                xc = x[:, c * LANES:(c + 1) * LANES]
                partner = jnp.where(first, pltpu.roll(xc, LANES - quarter, axis=1), pltpu.roll(xc, quarter, axis=1))
                outs.append(xc * cos + partner * sin)
            return jnp.concatenate(outs, axis=1)

        q = rot(q)
        k = rot(k)
    qo_ref[0] = (q * (SWA_HEAD_DIM ** -0.5)).astype(qo_ref.dtype)
    for g, kg in enumerate(_dup_groups(k)):
        kt_ref[0, g * LANES:(g + 1) * LANES, :] = kg.T.astype(kt_ref.dtype)
    vo_ref[0] = jnp.concatenate(_dup_groups(v_ref[0].astype(F32)), axis=1).astype(vo_ref.dtype)


def _swa_prep(p, q_col0, k_col0, v_col0, cos_t, sin_t, rope):
    bsz, t_len, _ = p.shape
    tr = _pick_tile(t_len, 512)
    return pl.pallas_call(
        functools.partial(_rope_kernel, rope=rope),
        out_shape=(jax.ShapeDtypeStruct((bsz, t_len, SWA_Q_W), BF16),
                   jax.ShapeDtypeStruct((bsz, 2 * SWA_KV_W, t_len), BF16),
                   jax.ShapeDtypeStruct((bsz, t_len, 2 * SWA_KV_W), BF16)),
        grid=(bsz, t_len // tr),
        in_specs=[pl.BlockSpec((1, tr, SWA_Q_W), lambda bi, i: (bi, i, q_col0 // SWA_Q_W)),
                  pl.BlockSpec((1, tr, SWA_KV_W), lambda bi, i: (bi, i, k_col0 // SWA_KV_W)),
                  pl.BlockSpec((1, tr, SWA_KV_W), lambda bi, i: (bi, i, v_col0 // SWA_KV_W)),
                  pl.BlockSpec((tr, LANES), lambda bi, i: (i, 0)),
                  pl.BlockSpec((tr, LANES), lambda bi, i: (i, 0))],
        out_specs=(pl.BlockSpec((1, tr, SWA_Q_W), lambda bi, i: (bi, i, 0)),
                   pl.BlockSpec((1, 2 * SWA_KV_W, tr), lambda bi, i: (bi, 0, i)),
                   pl.BlockSpec((1, tr, 2 * SWA_KV_W), lambda bi, i: (bi, i, 0))),
        compiler_params=_cparams("arbitrary", "arbitrary"),
        name="swa_prep",
    )(p, p, p, cos_t, sin_t)


def _rope_tables(t_len):
    rows = t_len // GRID_W
    row = jnp.broadcast_to(jnp.arange(rows)[:, None], (rows, GRID_W)).reshape(t_len).astype(F32)
    col = jnp.broadcast_to(jnp.arange(GRID_W)[None, :], (rows, GRID_W)).reshape(t_len).astype(F32)
    n_freq = SWA_HEAD_DIM // 4
    freq = jnp.power(ROPE_THETA, -jnp.arange(n_freq, dtype=F32) / n_freq)
    ang = jnp.concatenate([row[:, None] * freq, col[:, None] * freq], axis=-1)
    cos = jnp.cos(ang)
    sin = jnp.sin(ang)
    reps = LANES // SWA_HEAD_DIM
    return (jnp.tile(jnp.concatenate([cos, cos], axis=-1), (1, reps)),
            jnp.tile(jnp.concatenate([-sin, sin], axis=-1), (1, reps)))


SWA_SCORE_LEAD = 1


def _attend(q, key_blocks, sink_ref, o_ref):
    n_rows = q.shape[0]
    lane = lax.broadcasted_iota(jnp.int32, (n_rows, LANES), 1)
    lo = lane < LANES // 2
    hpg = SWA_Q_HEADS // SWA_KV_HEADS
    kt_all = jnp.concatenate([kt for kt, _, _ in key_blocks], axis=1)

    def group_scores(g):
        gs = slice(g * LANES, (g + 1) * LANES)
        q_parts, sink_parts = [], []
        for r in range(hpg):
            head = g * hpg + r
            qc = q[:, (head // 2) * LANES:(head // 2 + 1) * LANES]
            q_parts.append(jnp.where(lo if head % 2 == 0 else jnp.logical_not(lo), qc, jnp.zeros_like(qc)))
            sink_parts.append(jnp.broadcast_to(sink_ref[0:1, head:head + 1], (n_rows, 1)))
        q4 = jnp.concatenate(q_parts, axis=0)
        sink = jnp.concatenate(sink_parts, axis=0)
        s_all = _dot(q4, kt_all[gs, :])
        scores = []
        m_tile = None
        off = 0
        for _, v, mask in key_blocks:
            s = s_all[:, off:off + v.shape[0]]
            off += v.shape[0]
            if mask is not None:
                s = jnp.where(mask, s, NEG_BIG)
            scores.append(s)
            for t in range(s.shape[1] // LANES):
                st = s[:, t * LANES:(t + 1) * LANES]
                m_tile = st if m_tile is None else jnp.maximum(m_tile, st)
        return sink, scores, m_tile

    def group_finish(g, sink, scores, m_tile):
        gs = slice(g * LANES, (g + 1) * LANES)
        m = jnp.maximum(sink, jnp.max(m_tile, axis=-1, keepdims=True))
        pr = jnp.concatenate([jnp.exp(s - m).astype(BF16) for s in scores], axis=1)
        v_ones = jnp.concatenate([jnp.concatenate([v[:, gs], jnp.ones((v.shape[0], LANES), BF16)], axis=1)
                                  for _, v, _ in key_blocks], axis=0)
        acc = _dot(pr, v_ones)
        o = acc[:, :LANES] * (1.0 / (acc[:, LANES:] + jnp.exp(sink - m)))
        for pair in range(hpg // 2):
            col = (g * hpg) // 2 + pair
            even = o[(2 * pair) * n_rows:(2 * pair + 1) * n_rows]
            odd = o[(2 * pair + 1) * n_rows:(2 * pair + 2) * n_rows]
            o_ref[0, :, col * LANES:(col + 1) * LANES] = jnp.where(lo, even, odd).astype(o_ref.dtype)

    pending = [group_scores(g) for g in range(min(SWA_SCORE_LEAD, SWA_KV_HEADS))]
    for g in range(SWA_KV_HEADS):
        if g + SWA_SCORE_LEAD < SWA_KV_HEADS:
            pending.append(group_scores(g + SWA_SCORE_LEAD))
        group_finish(g, *pending.pop(0))


def _swa_latent_kernel(q_ref, kp_ref, kc_ref, kn_ref, vp_ref, vc_ref, vn_ref, kx_ref, vx_ref, sink_ref, o_ref):
    i = pl.program_id(1)
    nblk = pl.num_programs(1)
    blk = q_ref.shape[1]
    rows4 = (SWA_Q_HEADS // SWA_KV_HEADS) * blk
    qi = lax.broadcasted_iota(jnp.int32, (rows4, blk), 0) & (blk - 1)
    kj = lax.broadcasted_iota(jnp.int32, (rows4, blk), 1)
    mask_prev = kj >= qi + jnp.where(i > 0, 0, blk)
    mask_next = kj <= qi - jnp.where(i < nblk - 1, 0, blk)
    blocks = [(kp_ref[0], vp_ref[0], mask_prev), (kc_ref[0], vc_ref[0], None),
              (kn_ref[0], vn_ref[0], mask_next), (kx_ref[0], vx_ref[0], None)]
    _attend(q_ref[0], blocks, sink_ref, o_ref)


def _swa_latent(q, kt, v2, kt_c, v2_c, sinks):
    bsz, t_len, _ = q.shape
    l_len = v2_c.shape[1]
    blk = SWA_BLOCK
    assert blk & (blk - 1) == 0
    nblk = t_len // blk
    kvw = v2.shape[2]
    prev = lambda i: jnp.maximum(i - 1, 0)
    nxt = lambda i: jnp.minimum(i + 1, nblk - 1)

    def kspec(f):
        return pl.BlockSpec((1, kvw, blk), lambda bi, i: (bi, 0, f(i)))

    def vspec(f):
        return pl.BlockSpec((1, blk, kvw), lambda bi, i: (bi, f(i), 0))

    same = lambda i: i
    return pl.pallas_call(
        _swa_latent_kernel,
        out_shape=jax.ShapeDtypeStruct((bsz, t_len, SWA_Q_W), BF16),
        grid=(bsz, nblk),
        in_specs=[pl.BlockSpec((1, blk, SWA_Q_W), lambda bi, i: (bi, i, 0)),
                  kspec(prev), kspec(same), kspec(nxt), vspec(prev), vspec(same), vspec(nxt),
                  pl.BlockSpec((1, kvw, l_len), lambda bi, i: (bi, 0, 0)),
                  pl.BlockSpec((1, l_len, kvw), lambda bi, i: (bi, 0, 0)),
                  pl.BlockSpec((1, SWA_Q_HEADS), lambda bi, i: (0, 0))],
        out_specs=pl.BlockSpec((1, blk, SWA_Q_W), lambda bi, i: (bi, i, 0)),
        compiler_params=_cparams("arbitrary", "arbitrary"),
        name="swa_latent",
    )(q, kt, kt, kt, v2, v2, v2, kt_c, v2_c, sinks)


def _swa_context_kernel(q_ref, k_ref, v_ref, sink_ref, o_ref):
    _attend(q_ref[0], [(k_ref[0], v_ref[0], None)], sink_ref, o_ref)


def _swa_context(q, kt, v2, sinks):
    bsz, l_len, _ = q.shape
    kvw = v2.shape[2]
    return pl.pallas_call(
        _swa_context_kernel,
        out_shape=jax.ShapeDtypeStruct((bsz, l_len, SWA_Q_W), BF16),
        grid=(bsz,),
        in_specs=[pl.BlockSpec((1, l_len, SWA_Q_W), lambda bi: (bi, 0, 0)),
                  pl.BlockSpec((1, kvw, l_len), lambda bi: (bi, 0, 0)),
                  pl.BlockSpec((1, l_len, kvw), lambda bi: (bi, 0, 0)),
                  pl.BlockSpec((1, SWA_Q_HEADS), lambda bi: (0, 0))],
        out_specs=pl.BlockSpec((1, l_len, SWA_Q_W), lambda bi: (bi, 0, 0)),
        compiler_params=_cparams("arbitrary"),
        name="swa_context",
    )(q, kt, v2, sinks)


def _sgu_kernel(u_ref, v_ref, g_ref, b_ref, ws_ref, bs_ref, o_ref):
    n_rows = u_ref.shape[1]
    v = _layer_norm_rows(_gelu_tanh(v_ref[0].astype(F32)), g_ref[...], b_ref[...]).astype(BF16)
    cw = v.shape[1] // SGU_GROUPS
    for c in range(n_rows // SGU_CHUNK):
        rs = slice(c * SGU_CHUNK, (c + 1) * SGU_CHUNK)
        for g in range(SGU_GROUPS):
            cs = slice(g * cw, (g + 1) * cw)
            s = _dot(ws_ref[g], v[rs, cs]) + bs_ref[:, g:g + 1]
            o_ref[0, rs, cs] = (_gelu_tanh(u_ref[0, rs, cs].astype(F32)) * s).astype(o_ref.dtype)


def _sgu(p, u_col0, v_col0, ln_g, ln_b, ws, bs_t):
    bsz, t_len, _ = p.shape
    w = ln_g.shape[1]
    tr = _pick_tile(t_len, 2 * SGU_CHUNK)
    full = lambda bi, i: (0, 0)
    return pl.pallas_call(
        _sgu_kernel,
        out_shape=jax.ShapeDtypeStruct((bsz, t_len, w), BF16),
        grid=(bsz, t_len // tr),
        in_specs=[pl.BlockSpec((1, tr, w), lambda bi, i: (bi, i, u_col0 // w)),
                  pl.BlockSpec((1, tr, w), lambda bi, i: (bi, i, v_col0 // w)),
                  pl.BlockSpec((1, w), full), pl.BlockSpec((1, w), full),
                  pl.BlockSpec(ws.shape, lambda bi, i: (0, 0, 0)),
                  pl.BlockSpec(bs_t.shape, full)],
        out_specs=pl.BlockSpec((1, tr, w), lambda bi, i: (bi, i, 0)),
        compiler_params=_cparams("arbitrary", "arbitrary"),
        name="sgu",
    )(p, p, ln_g, ln_b, ws, bs_t)


def _merge_kernel(ya_ref, yb_ref, yc_ref, ga_ref, gb_ref, gc_ref, wa_ref, wb_ref, wc_ref, o_ref):
    y = (jax.nn.sigmoid(ga_ref[0].astype(F32)) * _dot(ya_ref[0], wa_ref[...])
         + jax.nn.sigmoid(gb_ref[0].astype(F32)) * _dot(yb_ref[0], wb_ref[...])
         + jax.nn.sigmoid(gc_ref[0].astype(F32)) * _dot(yc_ref[0], wc_ref[...]))
    o_ref[0] = y.astype(o_ref.dtype)


def _merge(ya, yb, yc, p, gate_col0, wa, wb, wc):
    bsz, t_len, _ = ya.shape
    d = wa.shape[1]
    tm = _pick_tile(t_len, 512)
    tn = d
    nb = d // tn
    g0 = gate_col0 // tn

    def yspec(width):
        return pl.BlockSpec((1, tm, width), lambda bi, i, j: (bi, i, 0))

    def gspec(k):
        return pl.BlockSpec((1, tm, tn), lambda bi, i, j: (bi, i, g0 + k * nb + j))

    def wspec(width):
        return pl.BlockSpec((width, tn), lambda bi, i, j: (0, j), pipeline_mode=pl.Buffered(1))

    return pl.pallas_call(
        _merge_kernel,
        out_shape=jax.ShapeDtypeStruct((bsz, t_len, d), BF16),
        grid=(bsz, t_len // tm, nb),
        in_specs=[yspec(ya.shape[2]), yspec(yb.shape[2]), yspec(yc.shape[2]),
                  gspec(0), gspec(1), gspec(2),
                  wspec(wa.shape[0]), wspec(wb.shape[0]), wspec(wc.shape[0])],
        out_specs=pl.BlockSpec((1, tm, tn), lambda bi, i, j: (bi, i, j)),
        compiler_params=_cparams("arbitrary", "arbitrary", "arbitrary"),
        name="merge",
    )(ya, yb, yc, p, p, p, wa, wb, wc)


def _outproj_kernel(y_ref, w_ref, x_ref, mod_ref, g_ref, b_ref, o_ref, *, alpha):
    o = _dot(y_ref[0], w_ref[...])
    r = alpha * x_ref[0] + mod_ref[0, 2:3, :] * o
    o_ref[0] = _layer_norm_rows(r, g_ref[...], b_ref[...])


def _outproj(y, w, x, mod, ln_g, ln_b, alpha):
    bsz, t_len, d = x.shape
    tm = _pick_tile(t_len, 512)
    per_batch = mod.shape[0] > 1
    full = lambda bi, i: (0, 0)
    return pl.pallas_call(
        functools.partial(_outproj_kernel, alpha=alpha),
        out_shape=jax.ShapeDtypeStruct((bsz, t_len, d), F32),
        grid=(bsz, t_len // tm),
        in_specs=[pl.BlockSpec((1, tm, d), lambda bi, i: (bi, i, 0)),
                  pl.BlockSpec((d, d), full, pipeline_mode=pl.Buffered(1)),
                  pl.BlockSpec((1, tm, d), lambda bi, i: (bi, i, 0)),
                  pl.BlockSpec((1, N_MOD, d), (lambda bi, i: (bi, 0, 0)) if per_batch else (lambda bi, i: (0, 0, 0))),
                  pl.BlockSpec((1, d), full), pl.BlockSpec((1, d), full)],
        out_specs=pl.BlockSpec((1, tm, d), lambda bi, i: (bi, i, 0)),
        compiler_params=_cparams("arbitrary", "arbitrary"),
        name="out_proj_ln",
    )(y, w, x, mod, ln_g, ln_b)


def _ffn_kernel(x_ref, mod_ref, w1_ref, w2_ref, g_ref, b_ref, o_ref, h_scr, *, alpha):
    j = pl.program_id(2)

    @pl.when(j == 0)
    def _():
        h_scr[...] = (x_ref[0] * (1.0 + mod_ref[0, 4:5, :]) + mod_ref[0, 3:4, :]).astype(BF16)
        o_ref[...] = jnp.zeros_like(o_ref)

    a = jnp.maximum(_dot(h_scr[...], w1_ref[...]), 0.0)
    o_ref[0] += _dot((a * a).astype(BF16), w2_ref[...])

    @pl.when(j == pl.num_programs(2) - 1)
    def _():
        r = alpha * x_ref[0] + mod_ref[0, 5:6, :] * o_ref[0]
        o_ref[0] = _layer_norm_rows(r, g_ref[...], b_ref[...])


def _ffn(x, mod, w1, w2, ln_g, ln_b, alpha):
    bsz, t_len, d = x.shape
    dff = w1.shape[1]
    tm = _pick_tile(t_len, 512)
    tf = _pick_tile(dff, 1024)
    per_batch = mod.shape[0] > 1
    full = lambda bi, i, j: (0, 0)
    return pl.pallas_call(
        functools.partial(_ffn_kernel, alpha=alpha),
        out_shape=jax.ShapeDtypeStruct((bsz, t_len, d), F32),
        grid=(bsz, t_len // tm, dff // tf),
        in_specs=[pl.BlockSpec((1, tm, d), lambda bi, i, j: (bi, i, 0)),
                  pl.BlockSpec((1, N_MOD, d), (lambda bi, i, j: (bi, 0, 0)) if per_batch else (lambda bi, i, j: (0, 0, 0))),
                  pl.BlockSpec((d, tf), lambda bi, i, j: (0, j)),
                  pl.BlockSpec((tf, d), lambda bi, i, j: (j, 0)),
                  pl.BlockSpec((1, d), full), pl.BlockSpec((1, d), full)],
        out_specs=pl.BlockSpec((1, tm, d), lambda bi, i, j: (bi, i, 0)),
        scratch_shapes=[pltpu.VMEM((tm, d), BF16)],
        compiler_params=_cparams("arbitrary", "arbitrary", "arbitrary"),
        name="ffn_ln",
    )(x, mod, w1, w2, ln_g, ln_b)


def _in_proj_layout(d_model):
    sgu_w = d_model // 2
    src = {}
    off = 0
    for name, width in (("qkv", GDN_QKV_W), ("z", GDN_V_W), ("gdn_gates", 4 * GDN_HEADS),
                        ("swa_q", SWA_Q_W), ("swa_k", SWA_KV_W), ("swa_v", SWA_KV_W),
                        ("sgu_u", sgu_w), ("sgu_v", sgu_w),
                        ("gate_a", d_model), ("gate_b", d_model), ("gate_c", d_model)):
        src[name] = (off, width)
        off += width
    order = ("gate_a", "gate_b", "gate_c", "z", "swa_q", "sgu_u", "sgu_v", "qkv", "swa_k", "swa_v")
    dst = {}
    off = 0
    for name in order:
        dst[name] = off
        assert off % min(src[name][1], 2048) == 0 or name == "qkv"
        off += src[name][1]
    return src, order, dst


def kernel(x, c, ctx, c_ctx, w_ada, b_ada, w_in, b_in, gdn_conv, gdn_a_log, gdn_dt_bias, gdn_norm, swa_sinks, sgu_ln_g, sgu_ln_b, sgu_w, sgu_b, w_branch_a, w_branch_b, w_branch_c, w_out, ln_mix_g, ln_mix_b, w_ff1, w_ff2, ln_ff_g, ln_ff_b):
    bsz, t_len, d = x.shape
    depth = w_ada.shape[0]
    alpha = (2 * depth) ** 0.25
    src, order, dst = _in_proj_layout(d)

    pad_rows = (-(bsz + 1)) % SUBLANES
    cc = jnp.concatenate([c, c_ctx[None, :], jnp.zeros((pad_rows, d), F32)], axis=0)
    mod_all = _ada_mod(cc, w_ada, b_ada).reshape(depth, bsz + 1 + pad_rows, N_MOD, d)
    cos_t, sin_t = _rope_tables(t_len)
    ones_t = jnp.ones((ctx.shape[1], LANES), F32)
    zeros_t = jnp.zeros((ctx.shape[1], LANES), F32)
    masks = _gdn_masks()
    tri = _gdn_tri()
    n_gate = 4 * GDN_HEADS
    lane_pad = LANES - n_gate

    xc = ctx
    for l in range(depth):
        need_ctx = l < depth - 1
        mod_l = mod_all[l, :bsz]
        mod_c = mod_all[l, bsz:bsz + 1]

        def cols(a, name):
            s0, wd = src[name]
            return a[..., s0:s0 + wd]

        w_main = jnp.concatenate([cols(w_in[l], n) for n in order], axis=1).astype(BF16)
        b_main = jnp.concatenate([cols(b_in[l], n) for n in order], axis=0)[None, :]
        w_g = jnp.pad(cols(w_in[l], "gdn_gates"), ((0, 0), (0, lane_pad))).astype(BF16)
        b_g = jnp.pad(cols(b_in[l], "gdn_gates"), (0, lane_pad))[None, :]
        decay_pad = (GATE_RAW_DECAY, LANES - 2 * GATE_RAW_DECAY)
        alog = jnp.pad(gdn_a_log[l].reshape(-1), decay_pad)[None, :]
        dtb = jnp.pad(gdn_dt_bias[l].reshape(-1), decay_pad)[None, :]

        p_l, st_l = _inproj(x, mod_l, w_main, b_main, w_g, b_g)
        l_len = xc.shape[1]
        flat = lambda a: a.reshape(1, bsz * l_len, a.shape[-1])
        p_c, st_c = (a.reshape(bsz, l_len, -1) for a in _inproj(flat(xc), mod_c, w_main, b_main, w_g, b_g))
        tab_l, rows_l = _gdn_gates(st_l, alog, dtb, tri)
        tab_c, rows_c = _gdn_gates(st_c, alog, dtb, tri)

        qkv_l = _gdn_conv(p_l, gdn_conv[l], dst["qkv"])
        qkv_c = _gdn_conv(p_c, gdn_conv[l], dst["qkv"])
        ya_l, ya_c = _gdn(qkv_l, qkv_c, tab_l, rows_l, tab_c, rows_c, masks, p_l, p_c, dst["z"],
                          gdn_norm[l][None, :])

        sinks = swa_sinks[l][None, :]
        q_l, kt_l, v2_l = _swa_prep(p_l, dst["swa_q"], dst["swa_k"], dst["swa_v"], cos_t, sin_t, True)
        q_c, kt_c, v2_c = _swa_prep(p_c, dst["swa_q"], dst["swa_k"], dst["swa_v"], ones_t, zeros_t, False)
        yb_l = _swa_latent(q_l, kt_l, v2_l, kt_c, v2_c, sinks)

        ws = sgu_w[l].astype(BF16)
        bs_t = sgu_b[l].T
        sg, sb = sgu_ln_g[l][None, :], sgu_ln_b[l][None, :]
        yc_l = _sgu(p_l, dst["sgu_u"], dst["sgu_v"], sg, sb, ws, bs_t)

        wa, wb, wc = w_branch_a[l].astype(BF16), w_branch_b[l].astype(BF16), w_branch_c[l].astype(BF16)
        wo = w_out[l].astype(BF16)
        w1, w2 = w_ff1[l].astype(BF16), w_ff2[l].astype(BF16)
        lmg, lmb = ln_mix_g[l][None, :], ln_mix_b[l][None, :]
        lfg, lfb = ln_ff_g[l][None, :], ln_ff_b[l][None, :]

        y_l = _merge(ya_l, yb_l, yc_l, p_l, dst["gate_a"], wa, wb, wc)
        x = _outproj(y_l, wo, x, mod_l, lmg, lmb, alpha)
        x = _ffn(x, mod_l, w1, w2, lfg, lfb, alpha)
        if need_ctx:
            yb_c = _swa_context(q_c, kt_c, v2_c, sinks)
            yc_c = _sgu(p_c, dst["sgu_u"], dst["sgu_v"], sg, sb, ws, bs_t)
            y_c = _merge(flat(ya_c), flat(yb_c), flat(yc_c), flat(p_c), dst["gate_a"], wa, wb, wc)
            xc_flat = _outproj(y_c, wo, flat(xc), mod_c, lmg, lmb, alpha)
            xc = _ffn(xc_flat, mod_c, w1, w2, lfg, lfb, alpha).reshape(bsz, l_len, d)
    return x
```

```python
import functools
import math

import jax
import jax.numpy as jnp
import numpy as np
from jax import lax
from jax.experimental import pallas as pl
from jax.experimental.pallas import tpu as pltpu

F32 = jnp.float32
BF16 = jnp.bfloat16

GRID_W = 64
GDN_HEADS = 8
GDN_DK = 128
GDN_DV = 128
GDN_CONV = 5
SWA_Q_HEADS = 16
SWA_KV_HEADS = 4
SWA_HEAD_DIM = 64
SWA_BLOCK = 128
ROPE_THETA = 10000.0
SGU_GROUPS = 8
SGU_CHUNK = 128
N_MOD = 6
LN_EPS = 1e-5
RMS_EPS = 1e-6

GDN_QK_W = GDN_HEADS * GDN_DK
GDN_V_W = GDN_HEADS * GDN_DV
GDN_QKV_W = 2 * GDN_QK_W + GDN_V_W
SWA_Q_W = SWA_Q_HEADS * SWA_HEAD_DIM
SWA_KV_W = SWA_KV_HEADS * SWA_HEAD_DIM

LANES = 128
SUBLANES = 8
VMEM_LIMIT_BYTES = 56 * 1024 * 1024

GDN_CHUNK = 128
GDN_SERIES_BLOCK = 16
assert GDN_CHUNK == GDN_DK == GDN_DV == LANES
NEG_BIG = -1e30


def _cparams(*sem):
    return pltpu.CompilerParams(dimension_semantics=sem, vmem_limit_bytes=VMEM_LIMIT_BYTES)


def _dot(a, b):
    return jnp.dot(a, b, preferred_element_type=F32)


def _dot_nt(a, b):
    return lax.dot_general(a, b, (((1,), (1,)), ((), ())), preferred_element_type=F32)


def _layer_norm_rows(r, g, b):
    mu = jnp.mean(r, axis=-1, keepdims=True)
    d = r - mu
    var = jnp.mean(d * d, axis=-1, keepdims=True)
    return d * lax.rsqrt(var + LN_EPS) * g + b


def _gelu_tanh(x):
    return x * (0.5 * (1.0 + jnp.tanh(math.sqrt(2.0 / math.pi) * (x + 0.044715 * (x * x * x)))))


def _pick_tile(n, pref):
    t = min(n, pref)
    while n % t:
        t //= 2
    return t


def _ada_kernel(c_ref, w_ref, b_ref, o_ref):
    c = c_ref[...]
    a = (c * jax.nn.sigmoid(c)).astype(BF16)
    o_ref[0] = _dot(a, w_ref[0].astype(BF16)) + b_ref[0]


def _ada_mod(cc, w_ada, b_ada):
    nl, d, n = w_ada.shape
    tn = _pick_tile(n, 1024)
    return pl.pallas_call(
        _ada_kernel,
        out_shape=jax.ShapeDtypeStruct((nl, cc.shape[0], n), F32),
        grid=(nl, n // tn),
        in_specs=[pl.BlockSpec(cc.shape, lambda l, j: (0, 0)),
                  pl.BlockSpec((1, d, tn), lambda l, j: (l, 0, j)),
                  pl.BlockSpec((1, 1, tn), lambda l, j: (l, 0, j))],
        out_specs=pl.BlockSpec((1, cc.shape[0], tn), lambda l, j: (l, 0, j)),
        compiler_params=_cparams("arbitrary", "arbitrary"),
        name="ada_mod",
    )(cc, w_ada, b_ada.reshape(nl, 1, n))


def _inproj_kernel(x_ref, mod_ref, w_ref, b_ref, wg_ref, bg_ref, o_ref, g_ref, h_scr):
    @pl.when(pl.program_id(2) == 0)
    def _():
        shift = mod_ref[0, 0:1, :]
        scale = mod_ref[0, 1:2, :]
        h_scr[...] = (x_ref[0] * (1.0 + scale) + shift).astype(BF16)
        g_ref[0] = _dot(h_scr[...], wg_ref[...]) + bg_ref[...]

    o_ref[0] = (_dot(h_scr[...], w_ref[...]) + b_ref[...]).astype(o_ref.dtype)


def _inproj(x, mod, w, b, wg, bg):
    bsz, t_len, d = x.shape
    n = w.shape[1]
    tm = _pick_tile(t_len, 1024)
    tn = _pick_tile(n, 2304)
    per_batch = mod.shape[0] > 1
    return pl.pallas_call(
        _inproj_kernel,
        out_shape=(jax.ShapeDtypeStruct((bsz, t_len, n), BF16),
                   jax.ShapeDtypeStruct((bsz, t_len, LANES), F32)),
        grid=(bsz, t_len // tm, n // tn),
        in_specs=[pl.BlockSpec((1, tm, d), lambda bi, i, j: (bi, i, 0)),
                  pl.BlockSpec((1, N_MOD, d), (lambda bi, i, j: (bi, 0, 0)) if per_batch else (lambda bi, i, j: (0, 0, 0))),
                  pl.BlockSpec((d, tn), lambda bi, i, j: (0, j)),
                  pl.BlockSpec((1, tn), lambda bi, i, j: (0, j)),
                  pl.BlockSpec((d, LANES), lambda bi, i, j: (0, 0)),
                  pl.BlockSpec((1, LANES), lambda bi, i, j: (0, 0))],
        out_specs=(pl.BlockSpec((1, tm, tn), lambda bi, i, j: (bi, i, j)),
                   pl.BlockSpec((1, tm, LANES), lambda bi, i, j: (bi, i, 0))),
        scratch_shapes=[pltpu.VMEM((tm, d), BF16)],
        compiler_params=_cparams("arbitrary", "arbitrary", "arbitrary"),
        name="in_proj",
    )(x, mod, w, b, wg, bg)


GATE_BETA, GATE_CUM, GATE_E, GATE_EK, GATE_ETOT = 0, 16, 32, 48, 64
DIR_STRIDE = GDN_HEADS
GATE_RAW_DECAY = 2 * GDN_HEADS


def _split3_bf16(x):
    hi = x.astype(BF16)
    r1 = x - hi.astype(F32)
    mid = r1.astype(BF16)
    lo = (r1 - mid.astype(F32)).astype(BF16)
    return hi, mid, lo


def _gdn_gate_kernel(st_ref, alog_ref, dtb_ref, tri_ref, cols_ref, rows_ref):
    c_len = GDN_CHUNK
    tm = st_ref.shape[1]
    st = st_ref[0]
    lane = lax.broadcasted_iota(jnp.int32, (c_len, LANES), 1)
    is_decay = (lane >= GATE_RAW_DECAY) & (lane < 2 * GATE_RAW_DECAY)
    tri = tri_ref[...]
    for c in range(tm // c_len):
        rs = slice(c * c_len, (c + 1) * c_len)
        stc = st[rs]
        beta = jax.nn.sigmoid(stc)
        xs = stc + dtb_ref[...]
        softplus = jnp.maximum(xs, 0.0) + jnp.log1p(jnp.exp(-jnp.abs(xs)))
        g = jnp.where(is_decay, -jnp.exp(alog_ref[...]) * softplus, 0.0)
        t3 = sum(_dot(tri, piece) for piece in _split3_bf16(g))
        tot = t3[2 * c_len:3 * c_len]
        cum = jnp.where(lane >= GATE_RAW_DECAY + DIR_STRIDE, t3[c_len:2 * c_len], t3[0:c_len])
        e = jnp.exp(cum)
        ek = jnp.exp(tot - cum)
        et = jnp.exp(tot)
        assert GATE_CUM == GATE_RAW_DECAY
        table = jnp.where(lane < GATE_CUM, beta,
                          jnp.where(lane < GATE_E, cum,
                                    jnp.where(lane < GATE_EK, pltpu.roll(e, GATE_E - GATE_RAW_DECAY, axis=1),
                                              jnp.where(lane < GATE_ETOT, pltpu.roll(ek, GATE_EK - GATE_RAW_DECAY, axis=1),
                                                        pltpu.roll(et, GATE_ETOT - GATE_RAW_DECAY, axis=1)))))
        for hh in range(GDN_HEADS):
            cols_ref[0, hh, rs, :] = pltpu.roll(table, LANES - hh, axis=1) if hh else table
        rows_ref[0, :, rs] = cum.T[GATE_RAW_DECAY:2 * GATE_RAW_DECAY]


def _gdn_gates(st, alog, dtb, tri):
    bsz, t_len, _ = st.shape
    tm = _pick_tile(t_len, 512)
    small = lambda bi, i: (0, 0)
    return pl.pallas_call(
        _gdn_gate_kernel,
        out_shape=(jax.ShapeDtypeStruct((bsz, GDN_HEADS, t_len, LANES), F32),
                   jax.ShapeDtypeStruct((bsz, 2 * GDN_HEADS, t_len), F32)),
        grid=(bsz, t_len // tm),
        in_specs=[pl.BlockSpec((1, tm, LANES), lambda bi, i: (bi, i, 0)),
                  pl.BlockSpec((1, LANES), small),
                  pl.BlockSpec((1, LANES), small),
                  pl.BlockSpec(tri.shape, small)],
        out_specs=(pl.BlockSpec((1, GDN_HEADS, tm, LANES), lambda bi, i: (bi, 0, i, 0)),
                   pl.BlockSpec((1, 2 * GDN_HEADS, tm), lambda bi, i: (bi, 0, i))),
        compiler_params=_cparams("arbitrary", "arbitrary"),
        name="gdn_gates",
    )(st, alog, dtb, tri)


CONV_ROWS = 256
CONV_HALO = SUBLANES


def _gdn_conv_kernel(x_ref, w_ref, o_ref, pad_scr):
    t_len = x_ref.shape[1]
    j = pl.program_id(1)
    zeros = jnp.zeros((CONV_HALO, LANES), F32)
    pad_scr[0:CONV_HALO, :] = zeros
    pad_scr[t_len + CONV_HALO:t_len + 2 * CONV_HALO, :] = zeros
    pad_scr[CONV_HALO:t_len + CONV_HALO, :] = x_ref[0].astype(F32)
    heads_qk = 2 * GDN_HEADS
    qk_scale = jnp.where(j < GDN_HEADS, GDN_DK ** -0.5, 1.0).astype(F32)
    w = w_ref[...]
    half = GDN_CONV // 2

    def conv_silu(c):
        r0 = pl.multiple_of(c * CONV_ROWS, CONV_ROWS)
        acc = jnp.zeros((CONV_ROWS, LANES), F32)
        for k in range(GDN_CONV):
            off = CONV_HALO - half + k
            acc = acc + pad_scr[pl.ds(r0 + off, CONV_ROWS), :] * w[k:k + 1, :]
        return r0, acc * jax.nn.sigmoid(acc)

    def body_qk(c, carry):
        r0, y = conv_silu(c)
        ss = jnp.sum(y * y, axis=-1, keepdims=True)
        o_ref[0, pl.ds(r0, CONV_ROWS), :] = (y * (lax.rsqrt(ss + RMS_EPS) * qk_scale)).astype(o_ref.dtype)
        return carry

    def body_v(c, carry):
        r0, y = conv_silu(c)
        o_ref[0, pl.ds(r0, CONV_ROWS), :] = y.astype(o_ref.dtype)
        return carry

    @pl.when(j < heads_qk)
    def _():
        lax.fori_loop(0, t_len // CONV_ROWS, body_qk, 0, unroll=2 if t_len // CONV_ROWS % 2 == 0 else 1)

    @pl.when(j >= heads_qk)
    def _():
        lax.fori_loop(0, t_len // CONV_ROWS, body_v, 0)


def _gdn_conv(p, conv_w, col0):
    bsz, t_len, _ = p.shape
    nblk = GDN_QKV_W // LANES
    cb = col0 // LANES
    return pl.pallas_call(
        _gdn_conv_kernel,
        out_shape=jax.ShapeDtypeStruct((bsz, t_len, GDN_QKV_W), BF16),
        grid=(bsz, nblk),
        in_specs=[pl.BlockSpec((1, t_len, LANES), lambda bi, j: (bi, 0, cb + j)),
                  pl.BlockSpec((GDN_CONV, LANES), lambda bi, j: (0, j))],
        out_specs=pl.BlockSpec((1, t_len, LANES), lambda bi, j: (bi, 0, j)),
        scratch_shapes=[pltpu.VMEM((t_len + 2 * CONV_HALO, LANES), F32)],
        compiler_params=_cparams("arbitrary", "arbitrary"),
        name="gdn_conv",
    )(p, conv_w)


M_EYE, M_INCL, M_NEG_STRICT, M_DIAG_BLOCK, M_MERGE = 0, 1, 3, 5, 6
GDN_MERGE_LEVELS = int(math.log2(GDN_CHUNK // GDN_SERIES_BLOCK))
GDN_WIDE_STEPS = int(math.log2(GDN_SERIES_BLOCK)) - 2
GDN_TILE = 1024
GATED_NORM_ROWS = 512


def _gdn_masks():
    c = GDN_CHUNK
    i = np.arange(c)[:, None]
    j = np.arange(c)[None, :]
    rows = [i == j, j <= i, j >= i, -(j < i).astype(np.float32), -(j > i).astype(np.float32),
            (i // GDN_SERIES_BLOCK) == (j // GDN_SERIES_BLOCK)]
    s = GDN_SERIES_BLOCK
    while s < c:
        rows.append(((i // (2 * s)) == (j // (2 * s))) & ((i // s) != (j // s)))
        s *= 2
    return jnp.asarray(np.stack([np.asarray(r, np.float32) for r in rows]))


def _gdn_tri():
    c = GDN_CHUNK
    i = np.arange(c)[:, None]
    j = np.arange(c)[None, :]
    return jnp.asarray(np.concatenate([j <= i, j >= i, np.ones((c, c), bool)], axis=0).astype(np.float32), BF16)


def _in_two_halves(fn, *lists):
    half = (len(lists[0]) + 1) // 2
    out = [fn(*args) for args in zip(*(l[:half] for l in lists))]
    yield
    out += [fn(*args) for args in zip(*(l[half:] for l in lists))]
    yield
    return out


def _paired_products(lhs, rhs):
    c = GDN_CHUNK
    zero = jnp.zeros((c, c), BF16)
    n = len(lhs)
    half = -(-n // 4) * 2
    out = []
    for lo, hi in ((0, min(half, n)), (min(half, n), n)):
        for k in range(lo, hi - 1, 2):
            diag = jnp.concatenate([jnp.concatenate([rhs[k], zero], axis=1),
                                    jnp.concatenate([zero, rhs[k + 1]], axis=1)], axis=0)
            both = _dot(jnp.concatenate([lhs[k], lhs[k + 1]], axis=1), diag)
            out += [both[:, :c], both[:, c:]]
        if (hi - lo) % 2:
            out.append(_dot(lhs[hi - 1], rhs[hi - 1]))
        yield
    return out


def _gdn_intra_stages(chains_in, m_ref, refs, slot):
    c = GDN_CHUNK
    u_ref, wq_ref, akt_ref, et_ref = refs
    gqs = yield from _in_two_halves(lambda ch: _dot_nt(jnp.concatenate([ch[1], ch[0]], axis=0), ch[1]),
                                    chains_in)
    eye = m_ref[M_EYE]
    chains = []
    for (q, k, v, table, cum_row, d, idx), gq in zip(chains_in, gqs):
        def col(base, table=table, d=d):
            lane = base + d * DIR_STRIDE
            return table[:, lane:lane + 1]

        beta_c, cum_c = col(GATE_BETA), col(GATE_CUM)
        incl = m_ref[M_INCL + d]
        ex = jnp.exp((cum_c - cum_row) * incl)
        neg_m = (gq[:c] * beta_c) * (ex * m_ref[M_NEG_STRICT + d])
        chains.append(dict(q=q, k=k, v=v, table=table, idx=idx, d=d, col=col, beta_c=beta_c,
                           ex_incl=ex * incl, qk=gq[c:], neg_m=neg_m))
    ps = [ch["neg_m"] * m_ref[M_DIAG_BLOCK] for ch in chains]
    sums = [eye + p for p in ps]
    pbs = [p.astype(BF16) for p in ps]
    ps = yield from _paired_products(pbs, pbs)
    for _ in range(GDN_WIDE_STEPS):
        pbs = [p.astype(BF16) for p in ps]
        boths = yield from _in_two_halves(
            lambda pb, s: _dot(pb, jnp.concatenate([pb, s.astype(BF16)], axis=1)), pbs, sums)
        ps = [both[:, :c] for both in boths]
        sums = [s + both[:, c:] for s, both in zip(sums, boths)]
    tails = yield from _paired_products([p.astype(BF16) for p in ps], [s.astype(BF16) for s in sums])
    invs = [s + tail for s, tail in zip(sums, tails)]
    for lvl in range(GDN_MERGE_LEVELS):
        ibs = [inv.astype(BF16) for inv in invs]
        offs = [(ch["neg_m"] * m_ref[M_MERGE + lvl]).astype(BF16) for ch in chains]
        halves = yield from _paired_products(ibs, offs)
        corrs = yield from _paired_products([half.astype(BF16) for half in halves], ibs)
        invs = [inv + corr for inv, corr in zip(invs, corrs)]
    rhss = []
    for ch in chains:
        kf = ch["k"].astype(F32)
        vf = ch["v"].astype(F32)
        e_c = ch["col"](GATE_E)
        rhss.append(jnp.concatenate([(vf * ch["beta_c"]).astype(BF16), (kf * (ch["beta_c"] * e_c)).astype(BF16)], axis=1))
    uws = yield from _in_two_halves(lambda inv, rhs: _dot(inv.astype(BF16), rhs), invs, rhss)
    for ch, uw in zip(chains, uws):
        d, idx = ch["d"], ch["idx"]
        a = (ch["qk"] * ch["ex_incl"]).astype(BF16)
        qd = (ch["q"].astype(F32) * ch["col"](GATE_E)).astype(BF16)
        kdt = (ch["k"].astype(F32) * ch["col"](GATE_EK)).T.astype(BF16)
        u_ref[slot, d, idx] = uw[:, :c].astype(BF16)
        wq_ref[slot, d, idx] = jnp.concatenate([uw[:, c:].astype(BF16), qd], axis=0)
        akt_ref[slot, d, idx] = jnp.concatenate([a, kdt], axis=0)
        lane = GATE_ETOT + d * DIR_STRIDE
        et_ref[slot, d, idx] = jnp.broadcast_to(ch["table"][0:1, lane:lane + 1], (SUBLANES, LANES))


GDN_INTRA_STAGES = 2 * (3 + GDN_WIDE_STEPS + 2 * GDN_MERGE_LEVELS + 1)


def _gdn_scan_stages(refs, slot, s_scrs, o_scr, first_chunks, n):
    c = GDN_CHUNK
    u_ref, wq_ref, akt_ref, et_ref = refs
    for j in range(n):
        idxs = (j, n - 1 - j)
        ss = [s_scrs[d][...] for d in range(2)]
        ws_qs = [_dot(wq_ref[slot, d, idxs[d]], ss[d].astype(BF16)) for d in range(2)]
        yield
        v_news = [(u_ref[slot, d, idxs[d]].astype(F32) - ws_qs[d][:c]).astype(BF16) for d in range(2)]
        av_kvs = [_dot(akt_ref[slot, d, idxs[d]], v_news[d]) for d in range(2)]
        yield
        for d in range(2):
            r0 = pl.multiple_of((first_chunks[d] + idxs[d]) * c, c)
            o_scr[pl.ds(r0, c), :] += ws_qs[d][c:] + av_kvs[d][:c]
            s_scrs[d][...] = ss[d] * et_ref[slot, d, idxs[d]][0:1, :] + av_kvs[d][c:]


def _drain(stages):
    for _ in stages:
        pass


def _interleave(main, side, n_main, n_side):
    done = 0
    for k, _ in enumerate(main, 1):
        target = min(n_side, -(-k * n_side // n_main))
        while done < target:
            next(side)
            done += 1
    _drain(side)


def _gated_norm_stages(o_scr, z_ref, nw, y_ref, row0, row1):
    rows = min(GATED_NORM_ROWS, max(row1 - row0, 1))
    for r0 in range(row0, row1, rows):
        o = o_scr[r0:r0 + rows, :]
        z = z_ref[0, r0:r0 + rows, :].astype(F32)
        on = o * lax.rsqrt(jnp.mean(o * o, axis=-1, keepdims=True) + RMS_EPS) * nw
        y_ref[0, r0:r0 + rows, :] = (on * (z * jax.nn.sigmoid(z))).astype(y_ref.dtype)
        yield


def _gdn_kernel(qf_ref, kf_ref, vf_ref, tf_ref, rf_ref, qb_ref, kb_ref, vb_ref, tb_ref, rb_ref,
                qc_ref, kc_ref, vc_ref, tc_ref, rfc_ref, rbc_ref,
                m_ref, zl_ref, zc_ref, nw_ref, yl_ref, yc_ref,
                ul, wql, aktl, etl, uc, wqc, aktc, etc, ol_scr, oc_scr, sf_scr, sb_scr):
    c = GDN_CHUNK
    i = pl.program_id(2)
    n_tiles = pl.num_programs(2) - 1
    cpt = qf_ref.shape[1] // c
    n_ctx = qc_ref.shape[1] // c
    s_scrs = (sf_scr, sb_scr)
    lat_refs = (ul, wql, aktl, etl)
    ctx_refs = (uc, wqc, aktc, etc)
    cur = lax.rem(i, 2)
    prev = 1 - cur

    def chains_of(dirs_refs, n):
        chains = []
        for d, (q_ref, k_ref, v_ref, t_ref, r_ref) in dirs_refs:
            for j in range(n):
                rs = slice(j * c, (j + 1) * c)
                chains.append((q_ref[0, rs, :], k_ref[0, rs, :], v_ref[0, rs, :], t_ref[0, 0, rs, :],
                               r_ref[0, 0, j:j + 1, :], d, j))
        return chains

    def intra(slot):
        dirs_refs = ((0, (qf_ref, kf_ref, vf_ref, tf_ref, rf_ref)), (1, (qb_ref, kb_ref, vb_ref, tb_ref, rb_ref)))
        return _gdn_intra_stages(chains_of(dirs_refs, cpt), m_ref, lat_refs, slot)

    def scan(slot, step):
        first_chunks = (step * cpt, (n_tiles - 1 - step) * cpt)
        return _gdn_scan_stages(lat_refs, slot, s_scrs, ol_scr, first_chunks, cpt)

    @pl.when(i == 0)
    def _():
        sf_scr[...] = jnp.zeros_like(sf_scr)
        sb_scr[...] = jnp.zeros_like(sb_scr)
        oc_scr[...] = jnp.zeros_like(oc_scr)
        ol_scr[...] = jnp.zeros_like(ol_scr)
        dirs_refs = ((0, (qc_ref, kc_ref, vc_ref, tc_ref, rfc_ref)), (1, (qc_ref, kc_ref, vc_ref, tc_ref, rbc_ref)))

        def context():
            yield from _gdn_intra_stages(chains_of(dirs_refs, n_ctx), m_ref, ctx_refs, 0)
            yield from _gdn_scan_stages(ctx_refs, 0, s_scrs, oc_scr, (0, 0), n_ctx)

        _interleave(intra(cur), context(), GDN_INTRA_STAGES, GDN_INTRA_STAGES + 2 * n_ctx)

    @pl.when((i > 0) & (i < n_tiles))
    def _():
        _interleave(intra(cur), scan(prev, i - 1), GDN_INTRA_STAGES, 2 * cpt)

    @pl.when(i == n_tiles)
    def _():
        nw = nw_ref[...]
        t_len = ol_scr.shape[0]
        edge = min(cpt * c, t_len // 2)
        middle = _gated_norm_stages(ol_scr, zl_ref, nw, yl_ref, edge, t_len - edge)
        _interleave(scan(prev, i - 1), middle, 2 * cpt, -(-(t_len - 2 * edge) // GATED_NORM_ROWS))
        _drain(_gated_norm_stages(ol_scr, zl_ref, nw, yl_ref, 0, edge))
        _drain(_gated_norm_stages(ol_scr, zl_ref, nw, yl_ref, t_len - edge, t_len))
        _drain(_gated_norm_stages(oc_scr, zc_ref, nw, yc_ref, 0, oc_scr.shape[0]))


def _gdn(qkv_l, qkv_c, tab_l, rows_l, tab_c, rows_c, masks, p_l, p_c, z_col0, norm_w):
    bsz, t_len, _ = qkv_l.shape
    l_len = qkv_c.shape[1]
    c = GDN_CHUNK
    nh = GDN_HEADS
    tb = _pick_tile(t_len, GDN_TILE)
    n_tiles = t_len // tb
    n_lat, n_ctx = t_len // c, l_len // c
    rows_l = rows_l.reshape(bsz, 2 * nh, n_lat, c)
    rows_c = rows_c.reshape(bsz, 2 * nh, n_ctx, c)
    zb = z_col0 // LANES
    tile_of = (lambda i: jnp.minimum(i, n_tiles - 1), lambda i: jnp.maximum(n_tiles - 1 - i, 0))

    def dir_specs(d):
        tile = tile_of[d]
        return ([pl.BlockSpec((1, tb, LANES), lambda bi, h, i, off=off: (bi, tile(i), off + h))
                 for off in (0, nh, 2 * nh)]
                + [pl.BlockSpec((1, 1, tb, LANES), lambda bi, h, i: (bi, h, tile(i), 0)),
                   pl.BlockSpec((1, 1, tb // c, c), lambda bi, h, i: (bi, d * nh + h, tile(i), 0))])

    def ctx_spec(col_off):
        return pl.BlockSpec((1, l_len, LANES), lambda bi, h, i: (bi, 0, col_off + h))

    def full_spec(n_rows, col_off):
        return pl.BlockSpec((1, n_rows, LANES), lambda bi, h, i: (bi, 0, col_off + h))

    in_specs = (dir_specs(0) + dir_specs(1)
                + [ctx_spec(0), ctx_spec(nh), ctx_spec(2 * nh),
                   pl.BlockSpec((1, 1, l_len, LANES), lambda bi, h, i: (bi, h, 0, 0)),
                   pl.BlockSpec((1, 1, n_ctx, c), lambda bi, h, i: (bi, h, 0, 0)),
                   pl.BlockSpec((1, 1, n_ctx, c), lambda bi, h, i: (bi, nh + h, 0, 0)),
                   pl.BlockSpec(masks.shape, lambda bi, h, i: (0, 0, 0)),
                   full_spec(t_len, zb), full_spec(l_len, zb),
                   pl.BlockSpec((1, LANES), lambda bi, h, i: (0, 0))])

    def scratch(n_slots, n_chunk):
        lead = (n_slots, 2, n_chunk)
        return [pltpu.VMEM(lead + (c, GDN_DV), BF16), pltpu.VMEM(lead + (2 * c, GDN_DK), BF16),
                pltpu.VMEM(lead + (2 * c, c), BF16), pltpu.VMEM(lead + (SUBLANES, LANES), F32)]

    lat = (qkv_l, qkv_l, qkv_l, tab_l, rows_l)
    return pl.pallas_call(
        _gdn_kernel,
        out_shape=(jax.ShapeDtypeStruct((bsz, t_len, GDN_V_W), BF16),
                   jax.ShapeDtypeStruct((bsz, l_len, GDN_V_W), BF16)),
        grid=(bsz, nh, n_tiles + 1),
        in_specs=in_specs,
        out_specs=(full_spec(t_len, 0), full_spec(l_len, 0)),
        scratch_shapes=scratch(2, tb // c) + scratch(1, n_ctx) + [
            pltpu.VMEM((t_len, GDN_DV), F32), pltpu.VMEM((l_len, GDN_DV), F32),
            pltpu.VMEM((GDN_DK, GDN_DV), F32), pltpu.VMEM((GDN_DK, GDN_DV), F32)],
        compiler_params=_cparams("arbitrary", "arbitrary", "arbitrary"),
        name="gdn_scan",
    )(*lat, *lat, qkv_c, qkv_c, qkv_c, tab_c, rows_c, rows_c, masks, p_l, p_c, norm_w)


def _dup_groups(x):
    lane = lax.broadcasted_iota(jnp.int32, (x.shape[0], LANES), 1)
    lo = lane < LANES // 2
    outs = []
    for c in range(x.shape[1] // LANES):
        xc = x[:, c * LANES:(c + 1) * LANES]
        xs = pltpu.roll(xc, LANES // 2, axis=1)
        outs.append(jnp.where(lo, xc, xs))
        outs.append(jnp.where(lo, xs, xc))
    return outs


def _rope_kernel(q_ref, k_ref, v_ref, cos_ref, sin_ref, qo_ref, kt_ref, vo_ref, *, rope):
    q = q_ref[0].astype(F32)
    k = k_ref[0].astype(F32)
    if rope:
        cos = cos_ref[...]
        sin = sin_ref[...]
        quarter = SWA_HEAD_DIM // 2
        lane = lax.broadcasted_iota(jnp.int32, cos.shape, 1)
        first = (lane % SWA_HEAD_DIM) < quarter

        def rot(x):
            outs = []
            for c in range(x.shape[1] // LANES):
                xc = x[:, c * LANES:(c + 1) * LANES]
                partner = jnp.where(first, pltpu.roll(xc, LANES - quarter, axis=1), pltpu.roll(xc, quarter, axis=1))
                outs.append(xc * cos + partner * sin)
            return jnp.concatenate(outs, axis=1)

        q = rot(q)
        k = rot(k)
    qo_ref[0] = (q * (SWA_HEAD_DIM ** -0.5)).astype(qo_ref.dtype)
    for g, kg in enumerate(_dup_groups(k)):
        kt_ref[0, g * LANES:(g + 1) * LANES, :] = kg.T.astype(kt_ref.dtype)
    vo_ref[0] = jnp.concatenate(_dup_groups(v_ref[0].astype(F32)), axis=1).astype(vo_ref.dtype)


def _swa_prep(p, q_col0, k_col0, v_col0, cos_t, sin_t, rope):
    bsz, t_len, _ = p.shape
    tr = _pick_tile(t_len, 512)
    return pl.pallas_call(
        functools.partial(_rope_kernel, rope=rope),
        out_shape=(jax.ShapeDtypeStruct((bsz, t_len, SWA_Q_W), BF16),
                   jax.ShapeDtypeStruct((bsz, 2 * SWA_KV_W, t_len), BF16),
                   jax.ShapeDtypeStruct((bsz, t_len, 2 * SWA_KV_W), BF16)),
        grid=(bsz, t_len // tr),
        in_specs=[pl.BlockSpec((1, tr, SWA_Q_W), lambda bi, i: (bi, i, q_col0 // SWA_Q_W)),
                  pl.BlockSpec((1, tr, SWA_KV_W), lambda bi, i: (bi, i, k_col0 // SWA_KV_W)),
                  pl.BlockSpec((1, tr, SWA_KV_W), lambda bi, i: (bi, i, v_col0 // SWA_KV_W)),
                  pl.BlockSpec((tr, LANES), lambda bi, i: (i, 0)),
                  pl.BlockSpec((tr, LANES), lambda bi, i: (i, 0))],
        out_specs=(pl.BlockSpec((1, tr, SWA_Q_W), lambda bi, i: (bi, i, 0)),
                   pl.BlockSpec((1, 2 * SWA_KV_W, tr), lambda bi, i: (bi, 0, i)),
                   pl.BlockSpec((1, tr, 2 * SWA_KV_W), lambda bi, i: (bi, i, 0))),
        compiler_params=_cparams("arbitrary", "arbitrary"),
        name="swa_prep",
    )(p, p, p, cos_t, sin_t)


def _rope_tables(t_len):
    rows = t_len // GRID_W
    row = jnp.broadcast_to(jnp.arange(rows)[:, None], (rows, GRID_W)).reshape(t_len).astype(F32)
    col = jnp.broadcast_to(jnp.arange(GRID_W)[None, :], (rows, GRID_W)).reshape(t_len).astype(F32)
    n_freq = SWA_HEAD_DIM // 4
    freq = jnp.power(ROPE_THETA, -jnp.arange(n_freq, dtype=F32) / n_freq)
    ang = jnp.concatenate([row[:, None] * freq, col[:, None] * freq], axis=-1)
    cos = jnp.cos(ang)
    sin = jnp.sin(ang)
    reps = LANES // SWA_HEAD_DIM
    return (jnp.tile(jnp.concatenate([cos, cos], axis=-1), (1, reps)),
            jnp.tile(jnp.concatenate([-sin, sin], axis=-1), (1, reps)))


SWA_SCORE_LEAD = 1


def _attend(q, key_blocks, sink_ref, o_ref):
    n_rows = q.shape[0]
    lane = lax.broadcasted_iota(jnp.int32, (n_rows, LANES), 1)
    lo = lane < LANES // 2
    hpg = SWA_Q_HEADS // SWA_KV_HEADS
    kt_all = jnp.concatenate([kt for kt, _, _ in key_blocks], axis=1)

    def group_scores(g):
        gs = slice(g * LANES, (g + 1) * LANES)
        q_parts, sink_parts = [], []
        for r in range(hpg):
            head = g * hpg + r
            qc = q[:, (head // 2) * LANES:(head // 2 + 1) * LANES]
            q_parts.append(jnp.where(lo if head % 2 == 0 else jnp.logical_not(lo), qc, jnp.zeros_like(qc)))
            sink_parts.append(jnp.broadcast_to(sink_ref[0:1, head:head + 1], (n_rows, 1)))
        q4 = jnp.concatenate(q_parts, axis=0)
        sink = jnp.concatenate(sink_parts, axis=0)
        s_all = _dot(q4, kt_all[gs, :])
        scores = []
        m_tile = None
        off = 0
        for _, v, mask in key_blocks:
            s = s_all[:, off:off + v.shape[0]]
            off += v.shape[0]
            if mask is not None:
                s = jnp.where(mask, s, NEG_BIG)
            scores.append(s)
            for t in range(s.shape[1] // LANES):
                st = s[:, t * LANES:(t + 1) * LANES]
                m_tile = st if m_tile is None else jnp.maximum(m_tile, st)
        return sink, scores, m_tile

    def group_finish(g, sink, scores, m_tile):
        gs = slice(g * LANES, (g + 1) * LANES)
        m = jnp.maximum(sink, jnp.max(m_tile, axis=-1, keepdims=True))
        pr = jnp.concatenate([jnp.exp(s - m).astype(BF16) for s in scores], axis=1)
        v_ones = jnp.concatenate([jnp.concatenate([v[:, gs], jnp.ones((v.shape[0], LANES), BF16)], axis=1)
                                  for _, v, _ in key_blocks], axis=0)
        acc = _dot(pr, v_ones)
        o = acc[:, :LANES] * (1.0 / (acc[:, LANES:] + jnp.exp(sink - m)))
        for pair in range(hpg // 2):
            col = (g * hpg) // 2 + pair
            even = o[(2 * pair) * n_rows:(2 * pair + 1) * n_rows]
            odd = o[(2 * pair + 1) * n_rows:(2 * pair + 2) * n_rows]
            o_ref[0, :, col * LANES:(col + 1) * LANES] = jnp.where(lo, even, odd).astype(o_ref.dtype)

    pending = [group_scores(g) for g in range(min(SWA_SCORE_LEAD, SWA_KV_HEADS))]
    for g in range(SWA_KV_HEADS):
        if g + SWA_SCORE_LEAD < SWA_KV_HEADS:
            pending.append(group_scores(g + SWA_SCORE_LEAD))
        group_finish(g, *pending.pop(0))


def _swa_latent_kernel(q_ref, kp_ref, kc_ref, kn_ref, vp_ref, vc_ref, vn_ref, kx_ref, vx_ref, sink_ref, o_ref):
    i = pl.program_id(1)
    nblk = pl.num_programs(1)
    blk = q_ref.shape[1]
    rows4 = (SWA_Q_HEADS // SWA_KV_HEADS) * blk
    qi = lax.broadcasted_iota(jnp.int32, (rows4, blk), 0) & (blk - 1)
    kj = lax.broadcasted_iota(jnp.int32, (rows4, blk), 1)
    mask_prev = kj >= qi + jnp.where(i > 0, 0, blk)
    mask_next = kj <= qi - jnp.where(i < nblk - 1, 0, blk)
    blocks = [(kp_ref[0], vp_ref[0], mask_prev), (kc_ref[0], vc_ref[0], None),
              (kn_ref[0], vn_ref[0], mask_next), (kx_ref[0], vx_ref[0], None)]
    _attend(q_ref[0], blocks, sink_ref, o_ref)


def _swa_latent(q, kt, v2, kt_c, v2_c, sinks):
    bsz, t_len, _ = q.shape
    l_len = v2_c.shape[1]
    blk = SWA_BLOCK
    assert blk & (blk - 1) == 0
    nblk = t_len // blk
    kvw = v2.shape[2]
    prev = lambda i: jnp.maximum(i - 1, 0)
    nxt = lambda i: jnp.minimum(i + 1, nblk - 1)

    def kspec(f):
        return pl.BlockSpec((1, kvw, blk), lambda bi, i: (bi, 0, f(i)))

    def vspec(f):
        return pl.BlockSpec((1, blk, kvw), lambda bi, i: (bi, f(i), 0))

    same = lambda i: i
    return pl.pallas_call(
        _swa_latent_kernel,
        out_shape=jax.ShapeDtypeStruct((bsz, t_len, SWA_Q_W), BF16),
        grid=(bsz, nblk),
        in_specs=[pl.BlockSpec((1, blk, SWA_Q_W), lambda bi, i: (bi, i, 0)),
                  kspec(prev), kspec(same), kspec(nxt), vspec(prev), vspec(same), vspec(nxt),
                  pl.BlockSpec((1, kvw, l_len), lambda bi, i: (bi, 0, 0)),
                  pl.BlockSpec((1, l_len, kvw), lambda bi, i: (bi, 0, 0)),
                  pl.BlockSpec((1, SWA_Q_HEADS), lambda bi, i: (0, 0))],
        out_specs=pl.BlockSpec((1, blk, SWA_Q_W), lambda bi, i: (bi, i, 0)),
        compiler_params=_cparams("arbitrary", "arbitrary"),
        name="swa_latent",
    )(q, kt, kt, kt, v2, v2, v2, kt_c, v2_c, sinks)


def _swa_context_kernel(q_ref, k_ref, v_ref, sink_ref, o_ref):
    _attend(q_ref[0], [(k_ref[0], v_ref[0], None)], sink_ref, o_ref)


def _swa_context(q, kt, v2, sinks):
    bsz, l_len, _ = q.shape
    kvw = v2.shape[2]
    return pl.pallas_call(
        _swa_context_kernel,
        out_shape=jax.ShapeDtypeStruct((bsz, l_len, SWA_Q_W), BF16),
        grid=(bsz,),
        in_specs=[pl.BlockSpec((1, l_len, SWA_Q_W), lambda bi: (bi, 0, 0)),
                  pl.BlockSpec((1, kvw, l_len), lambda bi: (bi, 0, 0)),
                  pl.BlockSpec((1, l_len, kvw), lambda bi: (bi, 0, 0)),
                  pl.BlockSpec((1, SWA_Q_HEADS), lambda bi: (0, 0))],
        out_specs=pl.BlockSpec((1, l_len, SWA_Q_W), lambda bi: (bi, 0, 0)),
        compiler_params=_cparams("arbitrary"),
        name="swa_context",
    )(q, kt, v2, sinks)


def _sgu_kernel(u_ref, v_ref, g_ref, b_ref, ws_ref, bs_ref, o_ref):
    n_rows = u_ref.shape[1]
    v = _layer_norm_rows(_gelu_tanh(v_ref[0].astype(F32)), g_ref[...], b_ref[...]).astype(BF16)
    cw = v.shape[1] // SGU_GROUPS
    for c in range(n_rows // SGU_CHUNK):
        rs = slice(c * SGU_CHUNK, (c + 1) * SGU_CHUNK)
        for g in range(SGU_GROUPS):
            cs = slice(g * cw, (g + 1) * cw)
            s = _dot(ws_ref[g], v[rs, cs]) + bs_ref[:, g:g + 1]
            o_ref[0, rs, cs] = (_gelu_tanh(u_ref[0, rs, cs].astype(F32)) * s).astype(o_ref.dtype)


def _sgu(p, u_col0, v_col0, ln_g, ln_b, ws, bs_t):
    bsz, t_len, _ = p.shape
    w = ln_g.shape[1]
    tr = _pick_tile(t_len, 2 * SGU_CHUNK)
    full = lambda bi, i: (0, 0)
    return pl.pallas_call(
        _sgu_kernel,
        out_shape=jax.ShapeDtypeStruct((bsz, t_len, w), BF16),
        grid=(bsz, t_len // tr),
        in_specs=[pl.BlockSpec((1, tr, w), lambda bi, i: (bi, i, u_col0 // w)),
                  pl.BlockSpec((1, tr, w), lambda bi, i: (bi, i, v_col0 // w)),
                  pl.BlockSpec((1, w), full), pl.BlockSpec((1, w), full),
                  pl.BlockSpec(ws.shape, lambda bi, i: (0, 0, 0)),
                  pl.BlockSpec(bs_t.shape, full)],
        out_specs=pl.BlockSpec((1, tr, w), lambda bi, i: (bi, i, 0)),
        compiler_params=_cparams("arbitrary", "arbitrary"),
        name="sgu",
    )(p, p, ln_g, ln_b, ws, bs_t)


def _merge_kernel(ya_ref, yb_ref, yc_ref, ga_ref, gb_ref, gc_ref, wa_ref, wb_ref, wc_ref, o_ref):
    y = (jax.nn.sigmoid(ga_ref[0].astype(F32)) * _dot(ya_ref[0], wa_ref[...])
         + jax.nn.sigmoid(gb_ref[0].astype(F32)) * _dot(yb_ref[0], wb_ref[...])
         + jax.nn.sigmoid(gc_ref[0].astype(F32)) * _dot(yc_ref[0], wc_ref[...]))
    o_ref[0] = y.astype(o_ref.dtype)


def _merge(ya, yb, yc, p, gate_col0, wa, wb, wc):
    bsz, t_len, _ = ya.shape
    d = wa.shape[1]
    tm = _pick_tile(t_len, 512)
    tn = d
    nb = d // tn
    g0 = gate_col0 // tn

    def yspec(width):
        return pl.BlockSpec((1, tm, width), lambda bi, i, j: (bi, i, 0))

    def gspec(k):
        return pl.BlockSpec((1, tm, tn), lambda bi, i, j: (bi, i, g0 + k * nb + j))

    def wspec(width):
        return pl.BlockSpec((width, tn), lambda bi, i, j: (0, j), pipeline_mode=pl.Buffered(1))

    return pl.pallas_call(
        _merge_kernel,
        out_shape=jax.ShapeDtypeStruct((bsz, t_len, d), BF16),
        grid=(bsz, t_len // tm, nb),
        in_specs=[yspec(ya.shape[2]), yspec(yb.shape[2]), yspec(yc.shape[2]),
                  gspec(0), gspec(1), gspec(2),
                  wspec(wa.shape[0]), wspec(wb.shape[0]), wspec(wc.shape[0])],
        out_specs=pl.BlockSpec((1, tm, tn), lambda bi, i, j: (bi, i, j)),
        compiler_params=_cparams("arbitrary", "arbitrary", "arbitrary"),
        name="merge",
    )(ya, yb, yc, p, p, p, wa, wb, wc)


def _outproj_kernel(y_ref, w_ref, x_ref, mod_ref, g_ref, b_ref, o_ref, *, alpha):
    o = _dot(y_ref[0], w_ref[...])
    r = alpha * x_ref[0] + mod_ref[0, 2:3, :] * o
    o_ref[0] = _layer_norm_rows(r, g_ref[...], b_ref[...])


def _outproj(y, w, x, mod, ln_g, ln_b, alpha):
    bsz, t_len, d = x.shape
    tm = _pick_tile(t_len, 512)
    per_batch = mod.shape[0] > 1
    full = lambda bi, i: (0, 0)
    return pl.pallas_call(
        functools.partial(_outproj_kernel, alpha=alpha),
        out_shape=jax.ShapeDtypeStruct((bsz, t_len, d), F32),
        grid=(bsz, t_len // tm),
        in_specs=[pl.BlockSpec((1, tm, d), lambda bi, i: (bi, i, 0)),
                  pl.BlockSpec((d, d), full, pipeline_mode=pl.Buffered(1)),
                  pl.BlockSpec((1, tm, d), lambda bi, i: (bi, i, 0)),
                  pl.BlockSpec((1, N_MOD, d), (lambda bi, i: (bi, 0, 0)) if per_batch else (lambda bi, i: (0, 0, 0))),
                  pl.BlockSpec((1, d), full), pl.BlockSpec((1, d), full)],
        out_specs=pl.BlockSpec((1, tm, d), lambda bi, i: (bi, i, 0)),
        compiler_params=_cparams("arbitrary", "arbitrary"),
        name="out_proj_ln",
    )(y, w, x, mod, ln_g, ln_b)


def _ffn_kernel(x_ref, mod_ref, w1_ref, w2_ref, g_ref, b_ref, o_ref, h_scr, *, alpha):
    j = pl.program_id(2)

    @pl.when(j == 0)
    def _():
        h_scr[...] = (x_ref[0] * (1.0 + mod_ref[0, 4:5, :]) + mod_ref[0, 3:4, :]).astype(BF16)
        o_ref[...] = jnp.zeros_like(o_ref)

    a = jnp.maximum(_dot(h_scr[...], w1_ref[...]), 0.0)
    o_ref[0] += _dot((a * a).astype(BF16), w2_ref[...])

    @pl.when(j == pl.num_programs(2) - 1)
    def _():
        r = alpha * x_ref[0] + mod_ref[0, 5:6, :] * o_ref[0]
        o_ref[0] = _layer_norm_rows(r, g_ref[...], b_ref[...])


def _ffn(x, mod, w1, w2, ln_g, ln_b, alpha):
    bsz, t_len, d = x.shape
    dff = w1.shape[1]
    tm = _pick_tile(t_len, 512)
    tf = _pick_tile(dff, 1024)
    per_batch = mod.shape[0] > 1
    full = lambda bi, i, j: (0, 0)
    return pl.pallas_call(
        functools.partial(_ffn_kernel, alpha=alpha),
        out_shape=jax.ShapeDtypeStruct((bsz, t_len, d), F32),
        grid=(bsz, t_len // tm, dff // tf),
        in_specs=[pl.BlockSpec((1, tm, d), lambda bi, i, j: (bi, i, 0)),
                  pl.BlockSpec((1, N_MOD, d), (lambda bi, i, j: (bi, 0, 0)) if per_batch else (lambda bi, i, j: (0, 0, 0))),
                  pl.BlockSpec((d, tf), lambda bi, i, j: (0, j)),
                  pl.BlockSpec((tf, d), lambda bi, i, j: (j, 0)),
                  pl.BlockSpec((1, d), full), pl.BlockSpec((1, d), full)],
        out_specs=pl.BlockSpec((1, tm, d), lambda bi, i, j: (bi, i, 0)),
        scratch_shapes=[pltpu.VMEM((tm, d), BF16)],
        compiler_params=_cparams("arbitrary", "arbitrary", "arbitrary"),
        name="ffn_ln",
    )(x, mod, w1, w2, ln_g, ln_b)


def _in_proj_layout(d_model):
    sgu_w = d_model // 2
    src = {}
    off = 0
    for name, width in (("qkv", GDN_QKV_W), ("z", GDN_V_W), ("gdn_gates", 4 * GDN_HEADS),
                        ("swa_q", SWA_Q_W), ("swa_k", SWA_KV_W), ("swa_v", SWA_KV_W),
                        ("sgu_u", sgu_w), ("sgu_v", sgu_w),
                        ("gate_a", d_model), ("gate_b", d_model), ("gate_c", d_model)):
        src[name] = (off, width)
        off += width
    order = ("gate_a", "gate_b", "gate_c", "z", "swa_q", "sgu_u", "sgu_v", "qkv", "swa_k", "swa_v")
    dst = {}
    off = 0
    for name in order:
        dst[name] = off
        assert off % min(src[name][1], 2048) == 0 or name == "qkv"
        off += src[name][1]
    return src, order, dst


def kernel(x, c, ctx, c_ctx, w_ada, b_ada, w_in, b_in, gdn_conv, gdn_a_log, gdn_dt_bias, gdn_norm, swa_sinks, sgu_ln_g, sgu_ln_b, sgu_w, sgu_b, w_branch_a, w_branch_b, w_branch_c, w_out, ln_mix_g, ln_mix_b, w_ff1, w_ff2, ln_ff_g, ln_ff_b):
    bsz, t_len, d = x.shape
    depth = w_ada.shape[0]
    alpha = (2 * depth) ** 0.25
    src, order, dst = _in_proj_layout(d)

    pad_rows = (-(bsz + 1)) % SUBLANES
    cc = jnp.concatenate([c, c_ctx[None, :], jnp.zeros((pad_rows, d), F32)], axis=0)
    mod_all = _ada_mod(cc, w_ada, b_ada).reshape(depth, bsz + 1 + pad_rows, N_MOD, d)
    cos_t, sin_t = _rope_tables(t_len)
    ones_t = jnp.ones((ctx.shape[1], LANES), F32)
    zeros_t = jnp.zeros((ctx.shape[1], LANES), F32)
    masks = _gdn_masks()
    tri = _gdn_tri()
    n_gate = 4 * GDN_HEADS
    lane_pad = LANES - n_gate

    xc = ctx
    for l in range(depth):
        need_ctx = l < depth - 1
        mod_l = mod_all[l, :bsz]
        mod_c = mod_all[l, bsz:bsz + 1]

        def cols(a, name):
            s0, wd = src[name]
            return a[..., s0:s0 + wd]

        w_main = jnp.concatenate([cols(w_in[l], n) for n in order], axis=1).astype(BF16)
        b_main = jnp.concatenate([cols(b_in[l], n) for n in order], axis=0)[None, :]
        w_g = jnp.pad(cols(w_in[l], "gdn_gates"), ((0, 0), (0, lane_pad))).astype(BF16)
        b_g = jnp.pad(cols(b_in[l], "gdn_gates"), (0, lane_pad))[None, :]
        decay_pad = (GATE_RAW_DECAY, LANES - 2 * GATE_RAW_DECAY)
        alog = jnp.pad(gdn_a_log[l].reshape(-1), decay_pad)[None, :]
        dtb = jnp.pad(gdn_dt_bias[l].reshape(-1), decay_pad)[None, :]

        p_l, st_l = _inproj(x, mod_l, w_main, b_main, w_g, b_g)
        l_len = xc.shape[1]
        flat = lambda a: a.reshape(1, bsz * l_len, a.shape[-1])
        p_c, st_c = (a.reshape(bsz, l_len, -1) for a in _inproj(flat(xc), mod_c, w_main, b_main, w_g, b_g))
        tab_l, rows_l = _gdn_gates(st_l, alog, dtb, tri)
        tab_c, rows_c = _gdn_gates(st_c, alog, dtb, tri)

        qkv_l = _gdn_conv(p_l, gdn_conv[l], dst["qkv"])
        qkv_c = _gdn_conv(p_c, gdn_conv[l], dst["qkv"])
        ya_l, ya_c = _gdn(qkv_l, qkv_c, tab_l, rows_l, tab_c, rows_c, masks, p_l, p_c, dst["z"],
                          gdn_norm[l][None, :])

        sinks = swa_sinks[l][None, :]
        q_l, kt_l, v2_l = _swa_prep(p_l, dst["swa_q"], dst["swa_k"], dst["swa_v"], cos_t, sin_t, True)
        q_c, kt_c, v2_c = _swa_prep(p_c, dst["swa_q"], dst["swa_k"], dst["swa_v"], ones_t, zeros_t, False)
        yb_l = _swa_latent(q_l, kt_l, v2_l, kt_c, v2_c, sinks)

        ws = sgu_w[l].astype(BF16)
        bs_t = sgu_b[l].T
        sg, sb = sgu_ln_g[l][None, :], sgu_ln_b[l][None, :]
        yc_l = _sgu(p_l, dst["sgu_u"], dst["sgu_v"], sg, sb, ws, bs_t)

        wa, wb, wc = w_branch_a[l].astype(BF16), w_branch_b[l].astype(BF16), w_branch_c[l].astype(BF16)
        wo = w_out[l].astype(BF16)
        w1, w2 = w_ff1[l].astype(BF16), w_ff2[l].astype(BF16)
        lmg, lmb = ln_mix_g[l][None, :], ln_mix_b[l][None, :]
        lfg, lfb = ln_ff_g[l][None, :], ln_ff_b[l][None, :]

        y_l = _merge(ya_l, yb_l, yc_l, p_l, dst["gate_a"], wa, wb, wc)
        x = _outproj(y_l, wo, x, mod_l, lmg, lmb, alpha)
        x = _ffn(x, mod_l, w1, w2, lfg, lfb, alpha)
        if need_ctx:
            yb_c = _swa_context(q_c, kt_c, v2_c, sinks)
            yc_c = _sgu(p_c, dst["sgu_u"], dst["sgu_v"], sg, sb, ws, bs_t)
            y_c = _merge(flat(ya_c), flat(yb_c), flat(yc_c), flat(p_c), dst["gate_a"], wa, wb, wc)
            xc_flat = _outproj(y_c, wo, flat(xc), mod_c, lmg, lmb, alpha)
            xc = _ffn(xc_flat, mod_c, w1, w2, lfg, lfb, alpha).reshape(bsz, l_len, d)
    return x
```
